```python
import jax, jax.numpy as jnp
from jax import lax
import numpy as np

D_MODEL = 1024
BATCH = 8
SEQ = 2048
DEPTH = 4
DEC_BATCH = 128
DEC_SEQ = 1
PAST_LEN = 16384
PAGE_SIZE = 128

D_MIX = D_MODEL
GLA_WIDTH = D_MIX // 2
CONV_CH = D_MIX - GLA_WIDTH
GLA_HEADS = 4
GLA_DV = GLA_WIDTH // GLA_HEADS
GLA_KEY_WIDTH = GLA_WIDTH // 2
GLA_DK = GLA_KEY_WIDTH // GLA_HEADS
GATE_RANK = 16
GATE_TAU = 16.0
GLA_CHUNK = 64
CONV_WIDTH = 31
FFN_CONV_WIDTH = 3
D_FF = ((8 * D_MODEL // 3 + 127) // 128) * 128
IN_COLS = 2 * GLA_KEY_WIDTH + 2 * GLA_WIDTH + GATE_RANK + 2 * CONV_CH
EPS = 1e-6

kernel_name = "hymba_gla_conformer_convffn_step"


def rms_norm(x, w):
    xf = x.astype(jnp.float32)
    y = xf * lax.rsqrt(jnp.mean(xf * xf, axis=-1, keepdims=True) + EPS)
    return (y * w.astype(jnp.float32)).astype(x.dtype)


def layer_norm(x, w, b):
    xf = x.astype(jnp.float32)
    mu = jnp.mean(xf, axis=-1, keepdims=True)
    var = jnp.mean(jnp.square(xf - mu), axis=-1, keepdims=True)
    y = (xf - mu) * lax.rsqrt(var + EPS)
    return (y * w.astype(jnp.float32) + b.astype(jnp.float32)).astype(x.dtype)


def causal_depthwise_conv(x, buf, w, b):
    width = w.shape[0]
    ch = x.shape[-1]
    xp = jnp.concatenate([buf.astype(x.dtype), x], axis=1)
    y = lax.conv_general_dilated(xp, w[:, None, :].astype(x.dtype), window_strides=(1,),
                                 padding='VALID', dimension_numbers=('NWC', 'WIO', 'NWC'),
                                 feature_group_count=ch)
    return y + b.astype(x.dtype), xp[:, xp.shape[1] - (width - 1):]


def gla_recurrence(q, k, v, log_a, s0):
    bsz, t = q.shape[0], q.shape[1]
    c = GLA_CHUNK if t % GLA_CHUNK == 0 else t
    n = t // c

    def blocks(a):
        return a.astype(jnp.float32).reshape(bsz, n, c, GLA_HEADS, a.shape[-1]).transpose(1, 0, 3, 2, 4)

    mask = jnp.tril(jnp.ones((c, c), dtype=bool))

    def step(S, inp):
        qc, kc, vc, gc = inp
        b = jnp.cumsum(gc, axis=2)
        b_last = b[:, :, -1:, :]
        q_dec = qc * jnp.exp(b)
        k_inv = kc * jnp.exp(-b)
        k_rem = kc * jnp.exp(b_last - b)
        scores = jnp.where(mask, jnp.einsum('bhtd,bhsd->bhts', q_dec, k_inv), 0.0)
        o = (jnp.einsum('bhtd,bhde->bhte', q_dec, S)
             + jnp.einsum('bhts,bhse->bhte', scores, vc))
        S = jnp.exp(b_last[:, :, 0, :, None]) * S + jnp.einsum('bhsd,bhse->bhde', k_rem, vc)
        return S, o

    S, o = lax.scan(step, s0.astype(jnp.float32), (blocks(q), blocks(k), blocks(v), blocks(log_a)))
    o = o.transpose(1, 0, 3, 2, 4).reshape(bsz, t, GLA_HEADS, GLA_DV)
    return o, S


def run_trunk(x, st_gla, st_conv, st_ffn, norm1_w, w_in, w_gate_up, b_gate, gla_norm_w,
              conv_w, conv_b, conv_ln_w, conv_ln_b, w_out, norm2_w, w_up, ffn_conv_w,
              ffn_conv_b, w_down, norm_f_w):
    bsz, t = x.shape[0], x.shape[1]
    splits = list(np.cumsum([GLA_KEY_WIDTH, GLA_KEY_WIDTH, GLA_WIDTH, GLA_WIDTH, GATE_RANK]))
    new_gla, new_conv, new_ffn = [], [], []
    for l in range(DEPTH):
        h = rms_norm(x, norm1_w[l])
        proj = h @ w_in[l].astype(x.dtype)
        q, k, v, g, a_low, u = jnp.split(proj, splits, axis=-1)
        log_a = jax.nn.log_sigmoid((a_low @ w_gate_up[l].astype(x.dtype)
                                    + b_gate[l].astype(x.dtype)).astype(jnp.float32)) / GATE_TAU
        q = q.reshape(bsz, t, GLA_HEADS, GLA_DK) * (GLA_DK ** -0.5)
        k = k.reshape(bsz, t, GLA_HEADS, GLA_DK)
        v = v.reshape(bsz, t, GLA_HEADS, GLA_DV)
        log_a = log_a.reshape(bsz, t, GLA_HEADS, GLA_DK)
        o, s_new = gla_recurrence(q, k, v, log_a, st_gla[l])
        o = o * lax.rsqrt(jnp.mean(o * o, axis=-1, keepdims=True) + EPS)
        o = o.reshape(bsz, t, GLA_WIDTH) * gla_norm_w[l].astype(jnp.float32)
        o_a = o.astype(x.dtype) * jax.nn.silu(g)
        ua, ub = jnp.split(u, 2, axis=-1)
        glu = ua * jax.nn.sigmoid(ub)
        c, c_buf = causal_depthwise_conv(glu, st_conv[l], conv_w[l], conv_b[l])
        o_b = jax.nn.silu(layer_norm(c, conv_ln_w[l], conv_ln_b[l]))
        x = x + jnp.concatenate([o_a, o_b], axis=-1) @ w_out[l].astype(x.dtype)
        h2 = rms_norm(x, norm2_w[l])
        up = h2 @ w_up[l].astype(x.dtype)
        upc, f_buf = causal_depthwise_conv(up, st_ffn[l], ffn_conv_w[l], ffn_conv_b[l])
        a, bgate = jnp.split(upc, 2, axis=-1)
        x = x + (jax.nn.silu(a) * bgate) @ w_down[l].astype(x.dtype)
        new_gla.append(s_new)
        new_conv.append(c_buf)
        new_ffn.append(f_buf)
    y = rms_norm(x, norm_f_w)
    return y, jnp.stack(new_gla), jnp.stack(new_conv), jnp.stack(new_ffn)


def setup_inputs(seed: int = 0) -> dict:
    key = jax.random.key(seed)
    ks = jax.random.split(key, 24)

    def nrm(k, shape, scale):
        return jax.random.normal(k, shape, jnp.float32) * scale

    return {
        "x_prompt": nrm(ks[0], (BATCH, SEQ, D_MODEL), 1.0),
        "x_sample": nrm(ks[1], (DEC_BATCH, DEC_SEQ, D_MODEL), 1.0),
        "state_gla": nrm(ks[2], (DEPTH, DEC_BATCH, GLA_HEADS, GLA_DK, GLA_DV), 0.5),
        "state_conv": nrm(ks[3], (DEPTH, DEC_BATCH, CONV_WIDTH - 1, CONV_CH), 0.5),
        "state_ffn": nrm(ks[4], (DEPTH, DEC_BATCH, FFN_CONV_WIDTH - 1, 2 * D_FF), 1.0),
        "norm1_w": 1.0 + nrm(ks[5], (DEPTH, D_MODEL), 0.02),
        "w_in": nrm(ks[6], (DEPTH, D_MODEL, IN_COLS), D_MODEL ** -0.5),
        "w_gate_up": nrm(ks[7], (DEPTH, GATE_RANK, GLA_KEY_WIDTH), GATE_RANK ** -0.5),
        "b_gate": nrm(ks[8], (DEPTH, GLA_KEY_WIDTH), 0.1),
        "gla_norm_w": 1.0 + nrm(ks[9], (DEPTH, GLA_WIDTH), 0.02),
        "conv_w": nrm(ks[10], (DEPTH, CONV_WIDTH, CONV_CH), CONV_WIDTH ** -0.5),
        "conv_b": nrm(ks[11], (DEPTH, CONV_CH), 0.02),
        "conv_ln_w": 1.0 + nrm(ks[12], (DEPTH, CONV_CH), 0.02),
        "conv_ln_b": nrm(ks[13], (DEPTH, CONV_CH), 0.02),
        "w_out": nrm(ks[14], (DEPTH, D_MIX, D_MODEL), D_MIX ** -0.5),
        "norm2_w": 1.0 + nrm(ks[15], (DEPTH, D_MODEL), 0.02),
        "w_up": nrm(ks[16], (DEPTH, D_MODEL, 2 * D_FF), D_MODEL ** -0.5),
        "ffn_conv_w": nrm(ks[17], (DEPTH, FFN_CONV_WIDTH, 2 * D_FF), FFN_CONV_WIDTH ** -0.5),
        "ffn_conv_b": nrm(ks[18], (DEPTH, 2 * D_FF), 0.02),
        "w_down": nrm(ks[19], (DEPTH, D_FF, D_MODEL), D_FF ** -0.5),
        "norm_f_w": 1.0 + nrm(ks[20], (D_MODEL,), 0.02),
    }


def reference(x_prompt, x_sample, state_gla, state_conv, state_ffn, norm1_w, w_in, w_gate_up,
              b_gate, gla_norm_w, conv_w, conv_b, conv_ln_w, conv_ln_b, w_out, norm2_w, w_up,
              ffn_conv_w, ffn_conv_b, w_down, norm_f_w):
    weights = (norm1_w, w_in, w_gate_up, b_gate, gla_norm_w, conv_w, conv_b, conv_ln_w,
               conv_ln_b, w_out, norm2_w, w_up, ffn_conv_w, ffn_conv_b, w_down, norm_f_w)
    bp = x_prompt.shape[0]
    zero_gla = jnp.zeros((DEPTH, bp, GLA_HEADS, GLA_DK, GLA_DV), jnp.float32)
    zero_conv = jnp.zeros((DEPTH, bp, CONV_WIDTH - 1, CONV_CH), x_prompt.dtype)
    zero_ffn = jnp.zeros((DEPTH, bp, FFN_CONV_WIDTH - 1, 2 * D_FF), x_prompt.dtype)
    y_prompt, gla_prompt, conv_prompt, ffn_prompt = run_trunk(
        x_prompt, zero_gla, zero_conv, zero_ffn, *weights)
    y_sample, gla_sample, conv_sample, ffn_sample = run_trunk(
        x_sample, state_gla, state_conv, state_ffn, *weights)
    return (y_prompt, y_sample, gla_prompt, gla_sample, conv_prompt, conv_sample, ffn_prompt, ffn_sample)
```

```python
import functools

import jax
import jax.numpy as jnp
from jax import lax
from jax.experimental import pallas as pl
from jax.experimental.pallas import tpu as pltpu

D_MODEL = 1024
GLA_WIDTH = 512
CONV_CH = 512
GLA_HEADS = 4
GLA_DV = 128
GLA_KEY_WIDTH = 256
GLA_DK = 64
GATE_RANK = 16
GATE_TAU = 16.0
GLA_CHUNK = 64
CONV_WIDTH = 31
FFN_CONV_WIDTH = 3
D_FF = 2816
EPS = 1e-6

LANES = 128
SUBLANES = 8

C_Q = 0
C_K = C_Q + GLA_KEY_WIDTH
C_V = C_K + GLA_KEY_WIDTH
C_G = C_V + GLA_WIDTH
C_U = C_G + GLA_WIDTH
C_A = C_U + 2 * CONV_CH
IN_PAD = C_A + LANES

CONV_HIST = 32
FFN_HIST = SUBLANES
PROMPT_TILE = 256
CONV_ROWS = 64
FFN_COLS = 256
SAMPLE_GLA_BLOCK = 16

VMEM_LIMIT = 56 * 1024 * 1024


def _rms(x, w):
    return x * lax.rsqrt(jnp.mean(x * x, axis=-1, keepdims=True) + EPS) * w


def _log_sigmoid(z):
    return jnp.minimum(z, 0.0) - jnp.log1p(jnp.exp(-jnp.abs(z)))


def _silu(x):
    return x * jax.nn.sigmoid(x)


def _bf(x):
    return x.astype(jnp.bfloat16)


def _dot(a, b):
    return jnp.dot(a, b, preferred_element_type=jnp.float32)


def _dot_nt(a, b):
    return lax.dot_general(a, b, (((1,), (1,)), ((), ())), preferred_element_type=jnp.float32)


def _split3(x):
    hi = _bf(x)
    r = x - hi.astype(jnp.float32)
    mid = _bf(r)
    lo = _bf(r - mid.astype(jnp.float32))
    return hi, mid, lo


def _log_decay(alow, wgu_ref, bg_ref):
    z = _dot(_bf(alow), wgu_ref[...]) + bg_ref[...]
    return _log_sigmoid(z) * (1.0 / GATE_TAU)


def _gla_chunk(q, k, v, g, st, tril, causal, head_masks):
    ghi, gmid, glo = _split3(g)
    b = _dot(tril, ghi) + _dot(tril, gmid) + _dot(tril, glo)
    b_last = b[GLA_CHUNK - 1:GLA_CHUNK, :]
    q_dec = (q * (GLA_DK ** -0.5)) * jnp.exp(b)
    k_inv = k * jnp.exp(-b)
    k_rem = k * jnp.exp(b_last - b)
    q_stack = _bf(jnp.concatenate([jnp.where(m, q_dec, 0.0) for m in head_masks], axis=0))
    k_stack = _bf(jnp.concatenate([jnp.where(m, k_rem, 0.0) for m in head_masks], axis=0))
    v_heads = [_bf(v[:, h * GLA_DV:(h + 1) * GLA_DV]) for h in range(GLA_HEADS)]
    scores = jnp.where(causal, _dot_nt(q_stack, _bf(k_inv)), 0.0)
    o = _dot_nt(q_stack, _bf(st))
    o_intra = [_dot(_bf(scores[h * GLA_CHUNK:(h + 1) * GLA_CHUNK]), v_heads[h])
               for h in range(GLA_HEADS)]
    o = o + jnp.concatenate(o_intra, axis=0)
    v_stack_t = _bf(jnp.concatenate([v[:, h * GLA_DV:(h + 1) * GLA_DV]
                                     for h in range(GLA_HEADS)], axis=0).T)
    st_new = jnp.exp(b_last) * st + _dot(v_stack_t, k_stack)
    return o, st_new


def _mixer_prompt_kernel(x_ref, n1_ref, win_ref, wgu_ref, bg_ref, gnw_ref, cw_ref, cb_ref,
                         lnw_ref, lnb_ref, wout_ref,
                         xo_ref, gla_ref, conv_ref,
                         proj_sc, st_sc, xp_sc, oab_sc, *, tile, n_tiles):
    i = pl.program_id(1)

    @pl.when(i == 0)
    def _():
        st_sc[...] = jnp.zeros_like(st_sc)
        xp_sc[:, 0:CONV_HIST, :] = jnp.zeros((CONV_CH // LANES, CONV_HIST, LANES), jnp.float32)

    x = x_ref[...]
    h = _bf(_rms(x, n1_ref[...]))
    proj_sc[...] = _dot(h, win_ref[...])

    row = lax.broadcasted_iota(jnp.int32, (GLA_CHUNK, GLA_CHUNK), 0)
    col = lax.broadcasted_iota(jnp.int32, (GLA_CHUNK, GLA_CHUNK), 1)
    tril = _bf(jnp.where(col <= row, 1.0, 0.0))
    srow = lax.broadcasted_iota(jnp.int32, (GLA_HEADS * GLA_CHUNK, GLA_CHUNK), 0)
    scol = lax.broadcasted_iota(jnp.int32, (GLA_HEADS * GLA_CHUNK, GLA_CHUNK), 1)
    causal = scol <= jnp.bitwise_and(srow, GLA_CHUNK - 1)
    lane = lax.broadcasted_iota(jnp.int32, (1, GLA_KEY_WIDTH), 1)
    head_masks = [jnp.logical_and(lane >= hh * GLA_DK, lane < (hh + 1) * GLA_DK)
                  for hh in range(GLA_HEADS)]

    st = st_sc[...]
    for c in range(tile // GLA_CHUNK):
        r0 = c * GLA_CHUNK
        rows = slice(r0, r0 + GLA_CHUNK)
        g = _log_decay(proj_sc[rows, C_A:C_A + LANES], wgu_ref, bg_ref)
        o, st = _gla_chunk(proj_sc[rows, C_Q:C_K], proj_sc[rows, C_K:C_V], proj_sc[rows, C_V:C_G],
                           g, st, tril, causal, head_masks)
        o = o * lax.rsqrt(jnp.mean(o * o, axis=-1, keepdims=True) + EPS)
        for hh in range(GLA_HEADS):
            cols = slice(hh * GLA_DV, (hh + 1) * GLA_DV)
            oh = o[hh * GLA_CHUNK:(hh + 1) * GLA_CHUNK] * gnw_ref[:, cols]
            gate = proj_sc[rows, C_G + hh * GLA_DV:C_G + (hh + 1) * GLA_DV]
            oab_sc[rows, cols] = _bf(oh * _silu(gate))
    st_sc[...] = st

    n_cb = CONV_CH // LANES
    glu = proj_sc[:, C_U:C_U + CONV_CH] * jax.nn.sigmoid(proj_sc[:, C_U + CONV_CH:C_A])
    for cb in range(n_cb):
        xp_sc[cb, CONV_HIST:CONV_HIST + tile, :] = glu[:, cb * LANES:(cb + 1) * LANES]
    first = CONV_HIST - (CONV_WIDTH - 1)
    for rb in range(tile // CONV_ROWS):
        r0 = rb * CONV_ROWS
        blocks = []
        for cb in range(n_cb):
            cols = slice(cb * LANES, (cb + 1) * LANES)
            acc = jnp.broadcast_to(cb_ref[:, cols], (CONV_ROWS, LANES))
            for j in range(CONV_WIDTH):
                acc = acc + cw_ref[j:j + 1, cols] * xp_sc[cb, r0 + first + j:r0 + first + j + CONV_ROWS, :]
            blocks.append(acc)
        cv = jnp.concatenate(blocks, axis=1)
        mu = jnp.mean(cv, axis=-1, keepdims=True)
        var = jnp.mean(jnp.square(cv - mu), axis=-1, keepdims=True)
        y = (cv - mu) * lax.rsqrt(var + EPS) * lnw_ref[...] + lnb_ref[...]
        oab_sc[r0:r0 + CONV_ROWS, GLA_WIDTH:GLA_WIDTH + CONV_CH] = _bf(_silu(y))
    xp_sc[:, 0:CONV_HIST, :] = xp_sc[:, tile:tile + CONV_HIST, :]

    xo_ref[...] = x + _dot(oab_sc[...], wout_ref[...])

    @pl.when(i == n_tiles - 1)
    def _():
        gla_ref[...] = st_sc[...].T.reshape(GLA_HEADS, GLA_DK, GLA_DV)
        for cb in range(n_cb):
            conv_ref[:, cb * LANES:(cb + 1) * LANES] = xp_sc[cb, first:CONV_HIST, :]


def _const_spec(shape, layer):
    nd = len(shape)
    return pl.BlockSpec((None,) + tuple(shape), lambda *_: (layer,) + (0,) * nd,
                        pipeline_mode=pl.Buffered(1))


def _mixer_prompt(x, layer, p):
    bsz, t, _ = x.shape
    tile = PROMPT_TILE
    n_tiles = t // tile
    kern = functools.partial(_mixer_prompt_kernel, tile=tile, n_tiles=n_tiles)
    return pl.pallas_call(
        kern,
        grid=(bsz, n_tiles),
        in_specs=[
            pl.BlockSpec((None, tile, D_MODEL), lambda b, i: (b, i, 0)),
            _const_spec((1, D_MODEL), layer),
            _const_spec((D_MODEL, IN_PAD), layer),
            _const_spec((LANES, GLA_KEY_WIDTH), layer),
            _const_spec((1, GLA_KEY_WIDTH), layer),
            _const_spec((1, GLA_WIDTH), layer),
            _const_spec((CONV_WIDTH, CONV_CH), layer),
            _const_spec((1, CONV_CH), layer),
            _const_spec((1, CONV_CH), layer),
            _const_spec((1, CONV_CH), layer),
            _const_spec((D_MODEL, D_MODEL), layer),
        ],
        out_specs=[
            pl.BlockSpec((None, tile, D_MODEL), lambda b, i: (b, i, 0)),
            pl.BlockSpec((None, GLA_HEADS, GLA_DK, GLA_DV), lambda b, i: (b, 0, 0, 0)),
            pl.BlockSpec((None, CONV_WIDTH - 1, CONV_CH), lambda b, i: (b, 0, 0)),
        ],
        out_shape=[
            jax.ShapeDtypeStruct((bsz, t, D_MODEL), jnp.float32),
            jax.ShapeDtypeStruct((bsz, GLA_HEADS, GLA_DK, GLA_DV), jnp.float32),
            jax.ShapeDtypeStruct((bsz, CONV_WIDTH - 1, CONV_CH), jnp.float32),
        ],
        scratch_shapes=[
            pltpu.VMEM((tile, IN_PAD), jnp.float32),
            pltpu.VMEM((GLA_DV, GLA_KEY_WIDTH), jnp.float32),
            pltpu.VMEM((CONV_CH // LANES, CONV_HIST + tile, LANES), jnp.float32),
            pltpu.VMEM((tile, D_MODEL), jnp.bfloat16),
        ],
        compiler_params=pltpu.CompilerParams(
            dimension_semantics=("arbitrary", "arbitrary"), vmem_limit_bytes=VMEM_LIMIT),
        name="mixer_prompt",
    )(x, p["norm1_w"], p["w_in"], p["w_gate_up"], p["b_gate"], p["gla_norm_w"], p["conv_w"],
      p["conv_b"], p["conv_ln_w"], p["conv_ln_b"], p["w_out"])


def _ffn_gate_block(up_ref, hist_rows, n_rows, c0, fw_ref, fb_ref):
    def conv(cols):
        acc = jnp.broadcast_to(fb_ref[:, cols], (n_rows, FFN_COLS))
        for j in range(FFN_CONV_WIDTH):
            s = hist_rows - (FFN_CONV_WIDTH - 1) + j
            acc = acc + fw_ref[j:j + 1, cols] * up_ref[s:s + n_rows, cols]
        return acc
    a = conv(slice(c0, c0 + FFN_COLS))
    bgate = conv(slice(D_FF + c0, D_FF + c0 + FFN_COLS))
    return _bf(_silu(a) * bgate)


def _ffn_prompt_kernel(x_ref, n2_ref, wup_ref, fw_ref, fb_ref, wdown_ref, nf_ref,
                       xo_ref, ffn_ref, up_sc, gated_sc, *, tile, n_tiles, final_norm):
    i = pl.program_id(1)

    @pl.when(i == 0)
    def _():
        up_sc[0:FFN_HIST, :] = jnp.zeros((FFN_HIST, 2 * D_FF), jnp.float32)

    x = x_ref[...]
    h2 = _bf(_rms(x, n2_ref[...]))
    up_sc[FFN_HIST:FFN_HIST + tile, :] = _dot(h2, wup_ref[...])
    for cblk in range(D_FF // FFN_COLS):
        c0 = cblk * FFN_COLS
        gated_sc[:, c0:c0 + FFN_COLS] = _ffn_gate_block(up_sc, FFN_HIST, tile, c0, fw_ref, fb_ref)
    y = x + _dot(gated_sc[...], wdown_ref[...])
    if final_norm:
        y = _rms(y, nf_ref[...])
    xo_ref[...] = y

    @pl.when(i == n_tiles - 1)
    def _():
        ffn_ref[...] = up_sc[FFN_HIST + tile - (FFN_CONV_WIDTH - 1):FFN_HIST + tile, :]

    up_sc[0:FFN_HIST, :] = up_sc[tile:tile + FFN_HIST, :]


def _ffn_prompt(x, layer, p, final_norm):
    bsz, t, _ = x.shape
    tile = PROMPT_TILE
    n_tiles = t // tile
    kern = functools.partial(_ffn_prompt_kernel, tile=tile, n_tiles=n_tiles, final_norm=final_norm)
    return pl.pallas_call(
        kern,
        grid=(bsz, n_tiles),
        in_specs=[
            pl.BlockSpec((None, tile, D_MODEL), lambda b, i: (b, i, 0)),
            _const_spec((1, D_MODEL), layer),
            _const_spec((D_MODEL, 2 * D_FF), layer),
            _const_spec((FFN_CONV_WIDTH, 2 * D_FF), layer),
            _const_spec((1, 2 * D_FF), layer),
            _const_spec((D_FF, D_MODEL), layer),
            pl.BlockSpec((1, D_MODEL), lambda b, i: (0, 0), pipeline_mode=pl.Buffered(1)),
        ],
        out_specs=[
            pl.BlockSpec((None, tile, D_MODEL), lambda b, i: (b, i, 0)),
            pl.BlockSpec((None, FFN_CONV_WIDTH - 1, 2 * D_FF), lambda b, i: (b, 0, 0)),
        ],
        out_shape=[
            jax.ShapeDtypeStruct((bsz, t, D_MODEL), jnp.float32),
            jax.ShapeDtypeStruct((bsz, FFN_CONV_WIDTH - 1, 2 * D_FF), jnp.float32),
        ],
        scratch_shapes=[
            pltpu.VMEM((FFN_HIST + tile, 2 * D_FF), jnp.float32),
            pltpu.VMEM((tile, D_FF), jnp.bfloat16),
        ],
        compiler_params=pltpu.CompilerParams(
            dimension_semantics=("arbitrary", "arbitrary"), vmem_limit_bytes=VMEM_LIMIT),
        name="ffn_prompt",
    )(x, p["norm2_w"], p["w_up"], p["ffn_conv_w"], p["ffn_conv_b"], p["w_down"], p["norm_f_w"])


def _proj_sample_kernel(x_ref, n1_ref, win_ref, wgu_ref, bg_ref, cs_ref, cw_ref, cb_ref,
                        lnw_ref, lnb_ref,
                        qd_ref, k_ref, a_ref, v_ref, sc_ref, g_ref, glu_ref, ob_ref):
    x = x_ref[...]
    proj = _dot(_bf(_rms(x, n1_ref[...])), win_ref[...])
    g = _log_decay(proj[:, C_A:C_A + LANES], wgu_ref, bg_ref)
    q = proj[:, C_Q:C_K] * (GLA_DK ** -0.5)
    k = proj[:, C_K:C_V]
    q_dec = q * jnp.exp(g)
    k_inv = k * jnp.exp(-g)
    qd_ref[...] = q_dec
    k_ref[...] = k * jnp.exp(g - g)
    a_ref[...] = jnp.exp(g)
    v_ref[...] = proj[:, C_V:C_G]
    g_ref[...] = proj[:, C_G:C_U]
    qk = q_dec * k_inv
    for hh in range(GLA_HEADS):
        sc_ref[:, hh * LANES:(hh + 1) * LANES] = jnp.broadcast_to(
            jnp.sum(qk[:, hh * GLA_DK:(hh + 1) * GLA_DK], axis=-1, keepdims=True),
            (x.shape[0], LANES))
    glu = proj[:, C_U:C_U + CONV_CH] * jax.nn.sigmoid(proj[:, C_U + CONV_CH:C_A])
    glu_ref[...] = glu
    acc = cb_ref[...] + cw_ref[CONV_WIDTH - 1:CONV_WIDTH, :] * glu
    for j in range(CONV_WIDTH - 1):
        acc = acc + cw_ref[j:j + 1, :] * cs_ref[j]
    mu = jnp.mean(acc, axis=-1, keepdims=True)
    var = jnp.mean(jnp.square(acc - mu), axis=-1, keepdims=True)
    y = (acc - mu) * lax.rsqrt(var + EPS) * lnw_ref[...] + lnb_ref[...]
    ob_ref[...] = _silu(y)


def _proj_sample(x, cs_t, layer, p):
    n = x.shape[0]
    full = lambda shape: pl.BlockSpec(shape, lambda i: (0,) * len(shape))
    f32 = jnp.float32
    return pl.pallas_call(
        _proj_sample_kernel,
        grid=(1,),
        in_specs=[
            full((n, D_MODEL)),
            _const_spec((1, D_MODEL), layer),
            _const_spec((D_MODEL, IN_PAD), layer),
            _const_spec((LANES, GLA_KEY_WIDTH), layer),
            _const_spec((1, GLA_KEY_WIDTH), layer),
            full((CONV_WIDTH - 1, n, CONV_CH)),
            _const_spec((CONV_WIDTH, CONV_CH), layer),
            _const_spec((1, CONV_CH), layer),
            _const_spec((1, CONV_CH), layer),
            _const_spec((1, CONV_CH), layer),
        ],
        out_specs=[full((n, GLA_KEY_WIDTH)), full((n, GLA_KEY_WIDTH)), full((n, GLA_KEY_WIDTH)),
                   full((n, GLA_WIDTH)), full((n, GLA_WIDTH)), full((n, GLA_WIDTH)),
                   full((n, CONV_CH)), full((n, CONV_CH))],
        out_shape=[jax.ShapeDtypeStruct((n, GLA_KEY_WIDTH), f32)] * 3
        + [jax.ShapeDtypeStruct((n, GLA_WIDTH), f32)] * 3
        + [jax.ShapeDtypeStruct((n, CONV_CH), f32)] * 2,
        compiler_params=pltpu.CompilerParams(
            dimension_semantics=("arbitrary",), vmem_limit_bytes=VMEM_LIMIT),
        name="proj_sample",
    )(x, p["norm1_w"], p["w_in"], p["w_gate_up"], p["b_gate"], cs_t, p["conv_w"], p["conv_b"],
      p["conv_ln_w"], p["conv_ln_b"])


def _gla_sample_kernel(qd_ref, k_ref, a_ref, v_ref, s_ref, o_ref, so_ref):
    def column(ref, bb):
        return jnp.broadcast_to(ref[bb], (LANES, GLA_KEY_WIDTH)).T

    def body(bb, carry):
        qc = column(qd_ref, bb)
        kc = column(k_ref, bb)
        ac = column(a_ref, bb)
        s = s_ref[bb].reshape(GLA_HEADS * GLA_DK, GLA_DV)
        v = v_ref[bb]
        vb = jnp.concatenate(
            [jnp.broadcast_to(v[:, hh * GLA_DV:(hh + 1) * GLA_DV], (GLA_DK, GLA_DV))
             for hh in range(GLA_HEADS)], axis=0)
        qs = qc * s
        o_ref[bb] = jnp.concatenate(
            [jnp.sum(qs[hh * GLA_DK:(hh + 1) * GLA_DK], axis=0, keepdims=True)
             for hh in range(GLA_HEADS)], axis=1)
        so_ref[bb] = (ac * s + kc * vb).reshape(GLA_HEADS, GLA_DK, GLA_DV)
        return carry

    lax.fori_loop(0, SAMPLE_GLA_BLOCK, body, 0)


def _gla_sample(qd, k, a, v, s):
    n = qd.shape[0]
    blk = SAMPLE_GLA_BLOCK
    vec = lambda w: pl.BlockSpec((blk, 1, w), lambda i: (i, 0, 0))
    sspec = pl.BlockSpec((blk, GLA_HEADS, GLA_DK, GLA_DV), lambda i: (i, 0, 0, 0))
    o, s_new = pl.pallas_call(
        _gla_sample_kernel,
        grid=(n // blk,),
        in_specs=[vec(GLA_KEY_WIDTH), vec(GLA_KEY_WIDTH), vec(GLA_KEY_WIDTH), vec(GLA_WIDTH), sspec],
        out_specs=[vec(GLA_WIDTH), sspec],
        out_shape=[jax.ShapeDtypeStruct((n, 1, GLA_WIDTH), jnp.float32),
                   jax.ShapeDtypeStruct(s.shape, jnp.float32)],
        compiler_params=pltpu.CompilerParams(
            dimension_semantics=("arbitrary",), vmem_limit_bytes=VMEM_LIMIT),
        name="gla_sample",
    )(qd[:, None, :], k[:, None, :], a[:, None, :], v[:, None, :], s)
    return o[:, 0, :], s_new


def _tail_sample_kernel(x_ref, oi_ref, sc_ref, v_ref, g_ref, ob_ref, gnw_ref, wout_ref,
                        n2_ref, wup_ref, st0_ref, st1_ref, fw_ref, fb_ref, wdown_ref, nf_ref,
                        xo_ref, up_ref, *, final_norm):
    n = x_ref.shape[0]
    parts = []
    for hh in range(GLA_HEADS):
        cols = slice(hh * GLA_DV, (hh + 1) * GLA_DV)
        o = oi_ref[:, cols] + sc_ref[:, cols] * v_ref[:, cols]
        o = o * lax.rsqrt(jnp.mean(o * o, axis=-1, keepdims=True) + EPS)
        parts.append(_bf(o * gnw_ref[:, cols] * _silu(g_ref[:, cols])))
    parts.append(_bf(ob_ref[...]))
    x = x_ref[...] + _dot(jnp.concatenate(parts, axis=1), wout_ref[...])

    h2 = _bf(_rms(x, n2_ref[...]))
    up = _dot(h2, wup_ref[...])
    up_ref[...] = up
    upc = (fb_ref[...] + fw_ref[0:1, :] * st0_ref[...] + fw_ref[1:2, :] * st1_ref[...]
           + fw_ref[2:3, :] * up)
    gated = _bf(_silu(upc[:, :D_FF]) * upc[:, D_FF:])
    y = x + _dot(gated, wdown_ref[...])
    if final_norm:
        y = _rms(y, nf_ref[...])
    xo_ref[...] = y


def _tail_sample(x, oi, sc, v, g, ob, st0, st1, layer, p, final_norm):
    n = x.shape[0]
    full = lambda shape: pl.BlockSpec(shape, lambda i: (0,) * len(shape))
    return pl.pallas_call(
        functools.partial(_tail_sample_kernel, final_norm=final_norm),
        grid=(1,),
        in_specs=[
            full((n, D_MODEL)), full((n, GLA_WIDTH)), full((n, GLA_WIDTH)), full((n, GLA_WIDTH)),
            full((n, GLA_WIDTH)), full((n, CONV_CH)),
            _const_spec((1, GLA_WIDTH), layer),
            _const_spec((D_MODEL, D_MODEL), layer),
            _const_spec((1, D_MODEL), layer),
            _const_spec((D_MODEL, 2 * D_FF), layer),
            full((n, 2 * D_FF)), full((n, 2 * D_FF)),
            _const_spec((FFN_CONV_WIDTH, 2 * D_FF), layer),
            _const_spec((1, 2 * D_FF), layer),
            _const_spec((D_FF, D_MODEL), layer),
            pl.BlockSpec((1, D_MODEL), lambda i: (0, 0)),
        ],
        out_specs=[full((n, D_MODEL)), full((n, 2 * D_FF))],
        out_shape=[jax.ShapeDtypeStruct((n, D_MODEL), jnp.float32),
                   jax.ShapeDtypeStruct((n, 2 * D_FF), jnp.float32)],
        compiler_params=pltpu.CompilerParams(
            dimension_semantics=("arbitrary",), vmem_limit_bytes=VMEM_LIMIT),
        name="tail_sample",
    )(x, oi, sc, v, g, ob, p["gla_norm_w"], p["w_out"], p["norm2_w"], p["w_up"], st0, st1,
      p["ffn_conv_w"], p["ffn_conv_b"], p["w_down"], p["norm_f_w"])


def _prepare_params(norm1_w, w_in, w_gate_up, b_gate, gla_norm_w, conv_w, conv_b, conv_ln_w,
                    conv_ln_b, w_out, norm2_w, w_up, ffn_conv_w, ffn_conv_b, w_down, norm_f_w):
    depth = w_in.shape[0]
    a0 = 2 * GLA_KEY_WIDTH + 2 * GLA_WIDTH
    w_in_r = jnp.concatenate(
        [w_in[:, :, :a0], w_in[:, :, a0 + GATE_RANK:], w_in[:, :, a0:a0 + GATE_RANK],
         jnp.zeros((depth, D_MODEL, LANES - GATE_RANK), w_in.dtype)], axis=2)
    wgu = jnp.concatenate(
        [w_gate_up, jnp.zeros((depth, LANES - GATE_RANK, GLA_KEY_WIDTH), w_gate_up.dtype)], axis=1)
    row = lambda a: a[:, None, :]
    return {
        "norm1_w": row(norm1_w), "w_in": _bf(w_in_r), "w_gate_up": _bf(wgu), "b_gate": row(b_gate),
        "gla_norm_w": row(gla_norm_w), "conv_w": conv_w, "conv_b": row(conv_b),
        "conv_ln_w": row(conv_ln_w), "conv_ln_b": row(conv_ln_b), "w_out": _bf(w_out),
        "norm2_w": row(norm2_w), "w_up": _bf(w_up), "ffn_conv_w": ffn_conv_w,
        "ffn_conv_b": row(ffn_conv_b), "w_down": _bf(w_down), "norm_f_w": norm_f_w[None, :],
    }


def kernel(x_prompt, x_sample, state_gla, state_conv, state_ffn, norm1_w, w_in, w_gate_up, b_gate,
           gla_norm_w, conv_w, conv_b, conv_ln_w, conv_ln_b, w_out, norm2_w, w_up, ffn_conv_w,
           ffn_conv_b, w_down, norm_f_w):
    p = _prepare_params(norm1_w, w_in, w_gate_up, b_gate, gla_norm_w, conv_w, conv_b, conv_ln_w,
                        conv_ln_b, w_out, norm2_w, w_up, ffn_conv_w, ffn_conv_b, w_down, norm_f_w)
    depth = w_in.shape[0]

    x = x_prompt
    gla_p, conv_p, ffn_p = [], [], []
    for l in range(depth):
        x, s_new, c_buf = _mixer_prompt(x, l, p)
        x, f_buf = _ffn_prompt(x, l, p, final_norm=(l == depth - 1))
        gla_p.append(s_new)
        conv_p.append(c_buf)
        ffn_p.append(f_buf)
    y_prompt = x

    xs = x_sample[:, 0, :]
    conv_t = jnp.swapaxes(state_conv, 1, 2)
    ffn_t = jnp.swapaxes(state_ffn, 1, 2)
    gla_s, conv_s, ffn_s = [], [], []
    for l in range(depth):
        qd, k, a, v, sc, g, glu, ob = _proj_sample(xs, conv_t[l], l, p)
        oi, s_new = _gla_sample(qd, k, a, v, state_gla[l])
        xs, up = _tail_sample(xs, oi, sc, v, g, ob, ffn_t[l, 0], ffn_t[l, 1], l, p,
                              final_norm=(l == depth - 1))
        gla_s.append(s_new)
        conv_s.append(jnp.concatenate([state_conv[l][:, 1:], glu[:, None, :]], axis=1))
        ffn_s.append(jnp.stack([state_ffn[l][:, 1], up], axis=1))
    y_sample = xs[:, None, :]

    return (y_prompt, y_sample, jnp.stack(gla_p), jnp.stack(gla_s), jnp.stack(conv_p),
            jnp.stack(conv_s), jnp.stack(ffn_p), jnp.stack(ffn_s))
```

```python
import functools

import jax
import jax.numpy as jnp
from jax import lax
from jax.experimental import pallas as pl
from jax.experimental.pallas import tpu as pltpu

D_MODEL = 1024
GLA_WIDTH = 512
CONV_CH = 512
GLA_HEADS = 4
GLA_DV = 128
GLA_KEY_WIDTH = 256
GLA_DK = 64
GATE_RANK = 16
GATE_TAU = 16.0
GLA_CHUNK = 64
CONV_WIDTH = 31
FFN_CONV_WIDTH = 3
D_FF = 2816
EPS = 1e-6

LANES = 128
SUBLANES = 8

CONV_BLOCKS = CONV_CH // LANES
C_U = 0
C_A = C_U + 2 * CONV_CH
C_Q = C_A + LANES
C_K = C_Q + GLA_KEY_WIDTH
C_V = C_K + GLA_KEY_WIDTH
C_G = C_V + GLA_WIDTH
IN_PAD = C_G + GLA_WIDTH

CONV_HIST = 32
FFN_HIST = SUBLANES
PROMPT_TILE = 256
CONV_ROWS = 64
SAMPLE_GLA_BLOCK = 16

VMEM_LIMIT = 56 * 1024 * 1024


def _rms(x, w):
    return x * lax.rsqrt(jnp.mean(x * x, axis=-1, keepdims=True) + EPS) * w


def _log_sigmoid(z):
    return jnp.minimum(z, 0.0) - jnp.log1p(jnp.exp(-jnp.abs(z)))


def _silu(x):
    return x * jax.nn.sigmoid(x)


def _bf(x):
    return x.astype(jnp.bfloat16)


def _dot(a, b):
    return jnp.dot(a, b, preferred_element_type=jnp.float32)


def _dot_nt(a, b):
    return lax.dot_general(a, b, (((1,), (1,)), ((), ())), preferred_element_type=jnp.float32)


def _split3(x):
    hi = _bf(x)
    r = x - hi.astype(jnp.float32)
    mid = _bf(r)
    lo = _bf(r - mid.astype(jnp.float32))
    return hi, mid, lo


def _log_decay(alow, wgu_ref, bg_ref):
    z = _dot(_bf(alow), wgu_ref[...]) + bg_ref[...]
    return _log_sigmoid(z) * (1.0 / GATE_TAU)


def _const_spec(shape, layer):
    nd = len(shape)
    return pl.BlockSpec((None,) + tuple(shape), lambda *_: (layer,) + (0,) * nd,
                        pipeline_mode=pl.Buffered(1))


CONV_FIRST = CONV_HIST - (CONV_WIDTH - 1)


def _conv_block(cb, xp_sc, cv_sc, cw_ref, cb_ref, tile):
    cols = slice(cb * LANES, (cb + 1) * LANES)
    for rb in range(tile // CONV_ROWS):
        r0 = rb * CONV_ROWS
        acc = jnp.broadcast_to(cb_ref[:, cols], (CONV_ROWS, LANES))
        for j in range(CONV_WIDTH):
            s = r0 + CONV_FIRST + j
            acc = acc + cw_ref[j:j + 1, cols] * xp_sc[cb, s:s + CONV_ROWS, :]
        cv_sc[cb, r0:r0 + CONV_ROWS, :] = acc


def _gla_tile(aqk_sc, v_sc, g_sc, wgu_ref, bg_ref, gnw_ref, st_sc, oab_sc, tile):
    q0 = C_Q - C_A
    k0 = C_K - C_A
    n_ch = tile // GLA_CHUNK
    trow = lax.broadcasted_iota(jnp.int32, (tile, tile), 0)
    tcol = lax.broadcasted_iota(jnp.int32, (tile, tile), 1)
    same_block = jnp.bitwise_and(trow, -GLA_CHUNK) == jnp.bitwise_and(tcol, -GLA_CHUNK)
    tril = _bf(jnp.where(jnp.logical_and(same_block, tcol <= trow), 1.0, 0.0))
    srow = lax.broadcasted_iota(jnp.int32, (GLA_HEADS * GLA_CHUNK, GLA_CHUNK), 0)
    scol = lax.broadcasted_iota(jnp.int32, (GLA_HEADS * GLA_CHUNK, GLA_CHUNK), 1)
    causal = scol <= jnp.bitwise_and(srow, GLA_CHUNK - 1)
    lane = lax.broadcasted_iota(jnp.int32, (1, GLA_KEY_WIDTH), 1)
    head_masks = [jnp.logical_and(lane >= hh * GLA_DK, lane < (hh + 1) * GLA_DK)
                  for hh in range(GLA_HEADS)]
    blocks = [slice(c * GLA_CHUNK, (c + 1) * GLA_CHUNK) for c in range(n_ch)]

    def stack_heads(a):
        return _bf(jnp.concatenate([jnp.where(m, a, 0.0) for m in head_masks], axis=0))

    g = _log_decay(aqk_sc[:, 0:LANES], wgu_ref, bg_ref)
    ghi, gmid, glo = _split3(g)
    b = _dot(tril, ghi) + _dot(tril, gmid) + _dot(tril, glo)
    last = [b[r.stop - 1:r.stop, :] for r in blocks]
    b_last = jnp.concatenate([jnp.broadcast_to(l, (GLA_CHUNK, GLA_KEY_WIDTH)) for l in last], axis=0)
    q = aqk_sc[:, q0:k0] * (GLA_DK ** -0.5)
    k = aqk_sc[:, k0:k0 + GLA_KEY_WIDTH]
    q_dec = q * jnp.exp(b)
    k_inv = _bf(k * jnp.exp(-b))
    k_rem = k * jnp.exp(b_last - b)

    q_stack = [stack_heads(q_dec[r]) for r in blocks]
    k_stack = [stack_heads(k_rem[r]) for r in blocks]
    scores = [jnp.where(causal, _dot_nt(q_stack[c], k_inv[blocks[c]]), 0.0) for c in range(n_ch)]
    v_blk = [v_sc[r, :] for r in blocks]
    v_heads = [[vb[:, hh * GLA_DV:(hh + 1) * GLA_DV] for hh in range(GLA_HEADS)] for vb in v_blk]
    kv = [_dot(_bf(jnp.concatenate(v_heads[c], axis=0).T), k_stack[c]) for c in range(n_ch)]
    o_intra = [jnp.concatenate(
        [_dot(_bf(scores[c][hh * GLA_CHUNK:(hh + 1) * GLA_CHUNK]), _bf(v_heads[c][hh]))
         for hh in range(GLA_HEADS)], axis=0) for c in range(n_ch)]

    st = st_sc[...]
    states = []
    for c in range(n_ch):
        states.append(_bf(st))
        st = jnp.exp(last[c]) * st + kv[c]
    st_sc[...] = st

    for c in range(n_ch):
        o = _dot_nt(q_stack[c], states[c]) + o_intra[c]
        o = o * lax.rsqrt(jnp.mean(o * o, axis=-1, keepdims=True) + EPS)
        for hh in range(GLA_HEADS):
            cols = slice(hh * GLA_DV, (hh + 1) * GLA_DV)
            oh = o[hh * GLA_CHUNK:(hh + 1) * GLA_CHUNK] * gnw_ref[:, cols]
            gate = g_sc[blocks[c], cols]
            oab_sc[blocks[c], cols] = _bf(oh * _silu(gate))


def _mixer_prompt_kernel(x_ref, n1_ref, win_ref, wgu_ref, bg_ref, gnw_ref, cw_ref, cb_ref,
                         lnw_ref, lnb_ref, wout_ref,
                         xo_ref, gla_ref, conv_ref,
                         aqk_sc, v_sc, g_sc, st_sc, xp_sc, oab_sc, h_sc, cv_sc, *, tile, n_tiles):
    i = pl.program_id(1)

    @pl.when(i == 0)
    def _():
        st_sc[...] = jnp.zeros_like(st_sc)
        xp_sc[:, 0:CONV_HIST, :] = jnp.zeros((CONV_BLOCKS, CONV_HIST, LANES), jnp.float32)

    x = x_ref[...]
    h_sc[...] = _bf(_rms(x, n1_ref[...]))

    def project(c0, c1):
        return _dot(h_sc[...], win_ref[:, c0:c1])

    def project_glu(cb):
        u = project(C_U + 2 * LANES * cb, C_U + 2 * LANES * (cb + 1))
        xp_sc[cb, CONV_HIST:CONV_HIST + tile, :] = u[:, :LANES] * jax.nn.sigmoid(u[:, LANES:])

    project_glu(0)
    project_glu(1)
    _conv_block(0, xp_sc, cv_sc, cw_ref, cb_ref, tile)
    project_glu(2)
    _conv_block(1, xp_sc, cv_sc, cw_ref, cb_ref, tile)
    project_glu(3)
    _conv_block(2, xp_sc, cv_sc, cw_ref, cb_ref, tile)
    aqk_sc[...] = project(C_A, C_V)
    _conv_block(3, xp_sc, cv_sc, cw_ref, cb_ref, tile)
    v_sc[...] = project(C_V, C_G)
    g_sc[...] = project(C_G, IN_PAD)

    cv = jnp.concatenate([cv_sc[cb] for cb in range(CONV_BLOCKS)], axis=1)
    mu = jnp.mean(cv, axis=-1, keepdims=True)
    var = jnp.mean(jnp.square(cv - mu), axis=-1, keepdims=True)
    y = (cv - mu) * lax.rsqrt(var + EPS) * lnw_ref[...] + lnb_ref[...]
    oab_sc[:, GLA_WIDTH:GLA_WIDTH + CONV_CH] = _bf(_silu(y))
    xp_sc[:, 0:CONV_HIST, :] = xp_sc[:, tile:tile + CONV_HIST, :]

    _gla_tile(aqk_sc, v_sc, g_sc, wgu_ref, bg_ref, gnw_ref, st_sc, oab_sc, tile)

    xo_ref[...] = x + _dot(oab_sc[...], wout_ref[...])

    @pl.when(i == n_tiles - 1)
    def _():
        gla_ref[...] = st_sc[...].T.reshape(GLA_HEADS, GLA_DK, GLA_DV)
        for cb in range(CONV_BLOCKS):
            conv_ref[:, cb * LANES:(cb + 1) * LANES] = xp_sc[cb, CONV_FIRST:CONV_HIST, :]


def _mixer_prompt(x, layer, p):
    bsz, t, _ = x.shape
    tile = PROMPT_TILE
    n_tiles = t // tile
    kern = functools.partial(_mixer_prompt_kernel, tile=tile, n_tiles=n_tiles)
    return pl.pallas_call(
        kern,
        grid=(bsz, n_tiles),
        in_specs=[
            pl.BlockSpec((None, tile, D_MODEL), lambda b, i: (b, i, 0)),
            _const_spec((1, D_MODEL), layer),
            _const_spec((D_MODEL, IN_PAD), layer),
            _const_spec((LANES, GLA_KEY_WIDTH), layer),
            _const_spec((1, GLA_KEY_WIDTH), layer),
            _const_spec((1, GLA_WIDTH), layer),
            _const_spec((CONV_WIDTH, CONV_CH), layer),
            _const_spec((1, CONV_CH), layer),
            _const_spec((1, CONV_CH), layer),
            _const_spec((1, CONV_CH), layer),
            _const_spec((D_MODEL, D_MODEL), layer),
        ],
        out_specs=[
            pl.BlockSpec((None, tile, D_MODEL), lambda b, i: (b, i, 0)),
            pl.BlockSpec((None, GLA_HEADS, GLA_DK, GLA_DV), lambda b, i: (b, 0, 0, 0)),
            pl.BlockSpec((None, CONV_WIDTH - 1, CONV_CH), lambda b, i: (b, 0, 0)),
        ],
        out_shape=[
            jax.ShapeDtypeStruct((bsz, t, D_MODEL), jnp.float32),
            jax.ShapeDtypeStruct((bsz, GLA_HEADS, GLA_DK, GLA_DV), jnp.float32),
            jax.ShapeDtypeStruct((bsz, CONV_WIDTH - 1, CONV_CH), jnp.float32),
        ],
        scratch_shapes=[
            pltpu.VMEM((tile, C_V - C_A), jnp.float32),
            pltpu.VMEM((tile, GLA_WIDTH), jnp.float32),
            pltpu.VMEM((tile, GLA_WIDTH), jnp.float32),
            pltpu.VMEM((GLA_DV, GLA_KEY_WIDTH), jnp.float32),
            pltpu.VMEM((CONV_BLOCKS, CONV_HIST + tile, LANES), jnp.float32),
            pltpu.VMEM((tile, D_MODEL), jnp.bfloat16),
            pltpu.VMEM((tile, D_MODEL), jnp.bfloat16),
            pltpu.VMEM((CONV_BLOCKS, tile, LANES), jnp.float32),
        ],
        compiler_params=pltpu.CompilerParams(
            dimension_semantics=("arbitrary", "arbitrary"), vmem_limit_bytes=VMEM_LIMIT),
        name="mixer_prompt",
    )(x, p["norm1_w"], p["w_in"], p["w_gate_up"], p["b_gate"], p["gla_norm_w"], p["conv_w"],
      p["conv_b"], p["conv_ln_w"], p["conv_ln_b"], p["w_out"])


FF_BLOCKS = D_FF // LANES


FF_GROUP = 2


def _ffn_gate(up_r, cb, fw_ref, fb_ref, tile):
    def conv(blk):
        cols = slice(blk * LANES, (blk + 1) * LANES)
        acc = jnp.broadcast_to(fb_ref[:, cols], (tile, LANES))
        for j in range(FFN_CONV_WIDTH):
            r = FFN_HIST - (FFN_CONV_WIDTH - 1) + j
            acc = acc + fw_ref[j:j + 1, cols] * up_r[blk, r:r + tile, :]
        return acc

    return _bf(_silu(conv(cb)) * conv(FF_BLOCKS + cb))


def _ffn_prompt_kernel(x_ref, n2_ref, wup_ref, fw_ref, fb_ref, wdown_ref, nf_ref,
                       xo_ref, ffn_ref, up_sc, h_sc, gated_sc, *, tile, n_tiles, final_norm):
    i = pl.program_id(1)

    @pl.when(i == 0)
    def _():
        up_sc[:, 0:FFN_HIST, :] = jnp.zeros((2 * FF_BLOCKS, FFN_HIST, LANES), jnp.float32)

    x = x_ref[...]
    h_sc[...] = _bf(_rms(x, n2_ref[...]))
    xo_ref[...] = x
    width = FF_GROUP * LANES
    n_groups = FF_BLOCKS // FF_GROUP

    def up_group(grp):
        for half in range(2):
            c0 = half * D_FF + grp * width
            up = _dot(h_sc[...], wup_ref[:, c0:c0 + width])
            for j in range(FF_GROUP):
                up_sc[half * FF_BLOCKS + grp * FF_GROUP + j, FFN_HIST:FFN_HIST + tile, :] = (
                    up[:, j * LANES:(j + 1) * LANES])

    def gate_group(grp):
        for j in range(FF_GROUP):
            cb = grp * FF_GROUP + j
            gated_sc[:, cb * LANES:(cb + 1) * LANES] = _ffn_gate(up_sc, cb, fw_ref, fb_ref, tile)

    def down_group(grp):
        rows = slice(grp * width, (grp + 1) * width)
        xo_ref[...] += _dot(gated_sc[:, rows], wdown_ref[rows, :])

    for step in range(n_groups + 2):
        if step < n_groups:
            up_group(step)
        if 1 <= step <= n_groups:
            gate_group(step - 1)
        if step >= 2:
            down_group(step - 2)
    if final_norm:
        xo_ref[...] = _rms(xo_ref[...], nf_ref[...])

    @pl.when(i == n_tiles - 1)
    def _():
        for cb in range(2 * FF_BLOCKS):
            ffn_ref[:, cb * LANES:(cb + 1) * LANES] = up_sc[
                cb, FFN_HIST + tile - (FFN_CONV_WIDTH - 1):FFN_HIST + tile, :]

    up_sc[:, 0:FFN_HIST, :] = up_sc[:, tile:tile + FFN_HIST, :]


def _ffn_prompt(x, layer, p, final_norm):
    bsz, t, _ = x.shape
    tile = PROMPT_TILE
    n_tiles = t // tile
    kern = functools.partial(_ffn_prompt_kernel, tile=tile, n_tiles=n_tiles, final_norm=final_norm)
    return pl.pallas_call(
        kern,
        grid=(bsz, n_tiles),
        in_specs=[
            pl.BlockSpec((None, tile, D_MODEL), lambda b, i: (b, i, 0)),
            _const_spec((1, D_MODEL), layer),
            _const_spec((D_MODEL, 2 * D_FF), layer),
            _const_spec((FFN_CONV_WIDTH, 2 * D_FF), layer),
            _const_spec((1, 2 * D_FF), layer),
            _const_spec((D_FF, D_MODEL), layer),
            pl.BlockSpec((1, D_MODEL), lambda b, i: (0, 0), pipeline_mode=pl.Buffered(1)),
        ],
        out_specs=[
            pl.BlockSpec((None, tile, D_MODEL), lambda b, i: (b, i, 0)),
            pl.BlockSpec((None, FFN_CONV_WIDTH - 1, 2 * D_FF), lambda b, i: (b, 0, 0)),
        ],
        out_shape=[
            jax.ShapeDtypeStruct((bsz, t, D_MODEL), jnp.float32),
            jax.ShapeDtypeStruct((bsz, FFN_CONV_WIDTH - 1, 2 * D_FF), jnp.float32),
        ],
        scratch_shapes=[
            pltpu.VMEM((2 * FF_BLOCKS, FFN_HIST + tile, LANES), jnp.float32),
            pltpu.VMEM((tile, D_MODEL), jnp.bfloat16),
            pltpu.VMEM((tile, D_FF), jnp.bfloat16),
        ],
        compiler_params=pltpu.CompilerParams(
            dimension_semantics=("arbitrary", "arbitrary"), vmem_limit_bytes=VMEM_LIMIT),
        name="ffn_prompt",
    )(x, p["norm2_w"], p["w_up"], p["ffn_conv_w"], p["ffn_conv_b"], p["w_down"], p["norm_f_w"])


def _proj_sample_kernel(x_ref, n1_ref, win_ref, wgu_ref, bg_ref, cs_ref, cw_ref, cb_ref,
                        lnw_ref, lnb_ref,
                        qd_ref, k_ref, a_ref, v_ref, sc_ref, g_ref, glu_ref, ob_ref):
    x = x_ref[...]
    proj = _dot(_bf(_rms(x, n1_ref[...])), win_ref[...])
    g = _log_decay(proj[:, C_A:C_A + LANES], wgu_ref, bg_ref)
    q = proj[:, C_Q:C_K] * (GLA_DK ** -0.5)
    k = proj[:, C_K:C_V]
    q_dec = q * jnp.exp(g)
    k_inv = k * jnp.exp(-g)
    qd_ref[...] = q_dec
    k_ref[...] = k * jnp.exp(g - g)
    a_ref[...] = jnp.exp(g)
    v_ref[...] = proj[:, C_V:C_G]
    g_ref[...] = proj[:, C_G:IN_PAD]
    qk = q_dec * k_inv
    for hh in range(GLA_HEADS):
        sc_ref[:, hh * LANES:(hh + 1) * LANES] = jnp.broadcast_to(
            jnp.sum(qk[:, hh * GLA_DK:(hh + 1) * GLA_DK], axis=-1, keepdims=True),
            (x.shape[0], LANES))
    glu = jnp.concatenate(
        [proj[:, C_U + 2 * LANES * cb:C_U + 2 * LANES * cb + LANES]
         * jax.nn.sigmoid(proj[:, C_U + 2 * LANES * cb + LANES:C_U + 2 * LANES * (cb + 1)])
         for cb in range(CONV_BLOCKS)], axis=1)
    glu_ref[...] = glu
    acc = cb_ref[...] + cw_ref[CONV_WIDTH - 1:CONV_WIDTH, :] * glu
    for j in range(CONV_WIDTH - 1):
        acc = acc + cw_ref[j:j + 1, :] * cs_ref[j]
    mu = jnp.mean(acc, axis=-1, keepdims=True)
    var = jnp.mean(jnp.square(acc - mu), axis=-1, keepdims=True)
    y = (acc - mu) * lax.rsqrt(var + EPS) * lnw_ref[...] + lnb_ref[...]
    ob_ref[...] = _silu(y)


def _proj_sample(x, cs_t, layer, p):
    n = x.shape[0]
    full = lambda shape: pl.BlockSpec(shape, lambda i: (0,) * len(shape))
    f32 = jnp.float32
    return pl.pallas_call(
        _proj_sample_kernel,
        grid=(1,),
        in_specs=[
            full((n, D_MODEL)),
            _const_spec((1, D_MODEL), layer),
            _const_spec((D_MODEL, IN_PAD), layer),
            _const_spec((LANES, GLA_KEY_WIDTH), layer),
            _const_spec((1, GLA_KEY_WIDTH), layer),
            full((CONV_WIDTH - 1, n, CONV_CH)),
            _const_spec((CONV_WIDTH, CONV_CH), layer),
            _const_spec((1, CONV_CH), layer),
            _const_spec((1, CONV_CH), layer),
            _const_spec((1, CONV_CH), layer),
        ],
        out_specs=[full((n, GLA_KEY_WIDTH)), full((n, GLA_KEY_WIDTH)), full((n, GLA_KEY_WIDTH)),
                   full((n, GLA_WIDTH)), full((n, GLA_WIDTH)), full((n, GLA_WIDTH)),
                   full((n, CONV_CH)), full((n, CONV_CH))],
        out_shape=[jax.ShapeDtypeStruct((n, GLA_KEY_WIDTH), f32)] * 3
        + [jax.ShapeDtypeStruct((n, GLA_WIDTH), f32)] * 3
        + [jax.ShapeDtypeStruct((n, CONV_CH), f32)] * 2,
        compiler_params=pltpu.CompilerParams(
            dimension_semantics=("arbitrary",), vmem_limit_bytes=VMEM_LIMIT),
        name="proj_sample",
    )(x, p["norm1_w"], p["w_in"], p["w_gate_up"], p["b_gate"], cs_t, p["conv_w"], p["conv_b"],
      p["conv_ln_w"], p["conv_ln_b"])


def _gla_sample_kernel(qd_ref, k_ref, a_ref, v_ref, s_ref, o_ref, so_ref):
    def column(ref, bb):
        return jnp.broadcast_to(ref[bb], (LANES, GLA_KEY_WIDTH)).T

    def body(bb, carry):
        qc = column(qd_ref, bb)
        kc = column(k_ref, bb)
        ac = column(a_ref, bb)
        s = s_ref[bb].reshape(GLA_HEADS * GLA_DK, GLA_DV)
        v = v_ref[bb]
        vb = jnp.concatenate(
            [jnp.broadcast_to(v[:, hh * GLA_DV:(hh + 1) * GLA_DV], (GLA_DK, GLA_DV))
             for hh in range(GLA_HEADS)], axis=0)
        qs = qc * s
        o_ref[bb] = jnp.concatenate(
            [jnp.sum(qs[hh * GLA_DK:(hh + 1) * GLA_DK], axis=0, keepdims=True)
             for hh in range(GLA_HEADS)], axis=1)
        so_ref[bb] = (ac * s + kc * vb).reshape(GLA_HEADS, GLA_DK, GLA_DV)
        return carry

    lax.fori_loop(0, SAMPLE_GLA_BLOCK, body, 0)


def _gla_sample(qd, k, a, v, s):
    n = qd.shape[0]
    blk = SAMPLE_GLA_BLOCK
    vec = lambda w: pl.BlockSpec((blk, 1, w), lambda i: (i, 0, 0))
    sspec = pl.BlockSpec((blk, GLA_HEADS, GLA_DK, GLA_DV), lambda i: (i, 0, 0, 0))
    o, s_new = pl.pallas_call(
        _gla_sample_kernel,
        grid=(n // blk,),
        in_specs=[vec(GLA_KEY_WIDTH), vec(GLA_KEY_WIDTH), vec(GLA_KEY_WIDTH), vec(GLA_WIDTH), sspec],
        out_specs=[vec(GLA_WIDTH), sspec],
        out_shape=[jax.ShapeDtypeStruct((n, 1, GLA_WIDTH), jnp.float32),
                   jax.ShapeDtypeStruct(s.shape, jnp.float32)],
        compiler_params=pltpu.CompilerParams(
            dimension_semantics=("arbitrary",), vmem_limit_bytes=VMEM_LIMIT),
        name="gla_sample",
    )(qd[:, None, :], k[:, None, :], a[:, None, :], v[:, None, :], s)
    return o[:, 0, :], s_new


def _tail_sample_kernel(x_ref, oi_ref, sc_ref, v_ref, g_ref, ob_ref, gnw_ref, wout_ref,
                        n2_ref, wup_ref, st0_ref, st1_ref, fw_ref, fb_ref, wdown_ref, nf_ref,
                        xo_ref, up_ref, *, final_norm):
    parts = []
    for hh in range(GLA_HEADS):
        cols = slice(hh * GLA_DV, (hh + 1) * GLA_DV)
        o = oi_ref[:, cols] + sc_ref[:, cols] * v_ref[:, cols]
        o = o * lax.rsqrt(jnp.mean(o * o, axis=-1, keepdims=True) + EPS)
        parts.append(_bf(o * gnw_ref[:, cols] * _silu(g_ref[:, cols])))
    parts.append(_bf(ob_ref[...]))
    x = x_ref[...] + _dot(jnp.concatenate(parts, axis=1), wout_ref[...])

    h2 = _bf(_rms(x, n2_ref[...]))
    up = _dot(h2, wup_ref[...])
    up_ref[...] = up
    upc = (fb_ref[...] + fw_ref[0:1, :] * st0_ref[...] + fw_ref[1:2, :] * st1_ref[...]
           + fw_ref[2:3, :] * up)
    gated = _bf(_silu(upc[:, :D_FF]) * upc[:, D_FF:])
    y = x + _dot(gated, wdown_ref[...])
    if final_norm:
        y = _rms(y, nf_ref[...])
    xo_ref[...] = y


def _tail_sample(x, oi, sc, v, g, ob, st0, st1, layer, p, final_norm):
    n = x.shape[0]
    full = lambda shape: pl.BlockSpec(shape, lambda i: (0,) * len(shape))
    return pl.pallas_call(
        functools.partial(_tail_sample_kernel, final_norm=final_norm),
        grid=(1,),
        in_specs=[
            full((n, D_MODEL)), full((n, GLA_WIDTH)), full((n, GLA_WIDTH)), full((n, GLA_WIDTH)),
            full((n, GLA_WIDTH)), full((n, CONV_CH)),
            _const_spec((1, GLA_WIDTH), layer),
            _const_spec((D_MODEL, D_MODEL), layer),
            _const_spec((1, D_MODEL), layer),
            _const_spec((D_MODEL, 2 * D_FF), layer),
            full((n, 2 * D_FF)), full((n, 2 * D_FF)),
            _const_spec((FFN_CONV_WIDTH, 2 * D_FF), layer),
            _const_spec((1, 2 * D_FF), layer),
            _const_spec((D_FF, D_MODEL), layer),
            pl.BlockSpec((1, D_MODEL), lambda i: (0, 0)),
        ],
        out_specs=[full((n, D_MODEL)), full((n, 2 * D_FF))],
        out_shape=[jax.ShapeDtypeStruct((n, D_MODEL), jnp.float32),
                   jax.ShapeDtypeStruct((n, 2 * D_FF), jnp.float32)],
        compiler_params=pltpu.CompilerParams(
            dimension_semantics=("arbitrary",), vmem_limit_bytes=VMEM_LIMIT),
        name="tail_sample",
    )(x, oi, sc, v, g, ob, p["gla_norm_w"], p["w_out"], p["norm2_w"], p["w_up"], st0, st1,
      p["ffn_conv_w"], p["ffn_conv_b"], p["w_down"], p["norm_f_w"])


def _prepare_params(norm1_w, w_in, w_gate_up, b_gate, gla_norm_w, conv_w, conv_b, conv_ln_w,
                    conv_ln_b, w_out, norm2_w, w_up, ffn_conv_w, ffn_conv_b, w_down, norm_f_w):
    depth = w_in.shape[0]
    a0 = 2 * GLA_KEY_WIDTH + 2 * GLA_WIDTH
    u0 = a0 + GATE_RANK
    pairs = []
    for cb in range(CONV_BLOCKS):
        pairs.append(w_in[:, :, u0 + cb * LANES:u0 + (cb + 1) * LANES])
        pairs.append(w_in[:, :, u0 + CONV_CH + cb * LANES:u0 + CONV_CH + (cb + 1) * LANES])
    w_in_r = jnp.concatenate(
        pairs + [w_in[:, :, a0:u0], jnp.zeros((depth, D_MODEL, LANES - GATE_RANK), w_in.dtype),
                 w_in[:, :, :a0]], axis=2)
    wgu = jnp.concatenate(
        [w_gate_up, jnp.zeros((depth, LANES - GATE_RANK, GLA_KEY_WIDTH), w_gate_up.dtype)], axis=1)
    row = lambda a: a[:, None, :]
    return {
        "norm1_w": row(norm1_w), "w_in": _bf(w_in_r), "w_gate_up": _bf(wgu), "b_gate": row(b_gate),
        "gla_norm_w": row(gla_norm_w), "conv_w": conv_w, "conv_b": row(conv_b),
        "conv_ln_w": row(conv_ln_w), "conv_ln_b": row(conv_ln_b), "w_out": _bf(w_out),
        "norm2_w": row(norm2_w), "w_up": _bf(w_up), "ffn_conv_w": ffn_conv_w,
        "ffn_conv_b": row(ffn_conv_b), "w_down": _bf(w_down), "norm_f_w": norm_f_w[None, :],
    }


def kernel(x_prompt, x_sample, state_gla, state_conv, state_ffn, norm1_w, w_in, w_gate_up, b_gate,
           gla_norm_w, conv_w, conv_b, conv_ln_w, conv_ln_b, w_out, norm2_w, w_up, ffn_conv_w,
           ffn_conv_b, w_down, norm_f_w):
    p = _prepare_params(norm1_w, w_in, w_gate_up, b_gate, gla_norm_w, conv_w, conv_b, conv_ln_w,
                        conv_ln_b, w_out, norm2_w, w_up, ffn_conv_w, ffn_conv_b, w_down, norm_f_w)
    depth = w_in.shape[0]

    x = x_prompt
    gla_p, conv_p, ffn_p = [], [], []
    for l in range(depth):
        x, s_new, c_buf = _mixer_prompt(x, l, p)
        x, f_buf = _ffn_prompt(x, l, p, final_norm=(l == depth - 1))
        gla_p.append(s_new)
        conv_p.append(c_buf)
        ffn_p.append(f_buf)
    y_prompt = x

    xs = x_sample[:, 0, :]
    conv_t = jnp.swapaxes(state_conv, 1, 2)
    ffn_t = jnp.swapaxes(state_ffn, 1, 2)
    gla_s, conv_s, ffn_s = [], [], []
    for l in range(depth):
        qd, k, a, v, sc, g, glu, ob = _proj_sample(xs, conv_t[l], l, p)
        oi, s_new = _gla_sample(qd, k, a, v, state_gla[l])
        xs, up = _tail_sample(xs, oi, sc, v, g, ob, ffn_t[l, 0], ffn_t[l, 1], l, p,
                              final_norm=(l == depth - 1))
        gla_s.append(s_new)
        conv_s.append(jnp.concatenate([state_conv[l][:, 1:], glu[:, None, :]], axis=1))
        ffn_s.append(jnp.stack([state_ffn[l][:, 1], up], axis=1))
    y_sample = xs[:, None, :]

    return (y_prompt, y_sample, jnp.stack(gla_p), jnp.stack(gla_s), jnp.stack(conv_p),
            jnp.stack(conv_s), jnp.stack(ffn_p), jnp.stack(ffn_s))
```

```python
import functools

import jax
import jax.numpy as jnp
from jax import lax
from jax.experimental import pallas as pl
from jax.experimental.pallas import tpu as pltpu

D_MODEL = 1024
GLA_WIDTH = 512
CONV_CH = 512
GLA_HEADS = 4
GLA_DV = 128
GLA_KEY_WIDTH = 256
GLA_DK = 64
GATE_RANK = 16
GATE_TAU = 16.0
GLA_CHUNK = 64
CONV_WIDTH = 31
FFN_CONV_WIDTH = 3
D_FF = 2816
EPS = 1e-6

LANES = 128
SUBLANES = 8

CONV_BLOCKS = CONV_CH // LANES
C_U = 0
C_A = C_U + 2 * CONV_CH
C_Q = C_A + LANES
C_K = C_Q + GLA_KEY_WIDTH
C_V = C_K + GLA_KEY_WIDTH
C_G = C_V + GLA_WIDTH
IN_PAD = C_G + GLA_WIDTH

CONV_HIST = 32
FFN_HIST = SUBLANES
PROMPT_TILE = 512
CUMSUM_ROWS = 256
CONV_ROWS = 64
SAMPLE_GLA_BLOCK = 16

VMEM_LIMIT = 56 * 1024 * 1024


def _rms(x, w):
    return x * lax.rsqrt(jnp.mean(x * x, axis=-1, keepdims=True) + EPS) * w


def _log_sigmoid(z):
    return jnp.minimum(z, 0.0) - jnp.log1p(jnp.exp(-jnp.abs(z)))


def _silu(x):
    return x * jax.nn.sigmoid(x)


def _bf(x):
    return x.astype(jnp.bfloat16)


def _dot(a, b):
    return jnp.dot(a, b, preferred_element_type=jnp.float32)


def _dot_nt(a, b):
    return lax.dot_general(a, b, (((1,), (1,)), ((), ())), preferred_element_type=jnp.float32)


def _split3(x):
    hi = _bf(x)
    r = x - hi.astype(jnp.float32)
    mid = _bf(r)
    lo = _bf(r - mid.astype(jnp.float32))
    return hi, mid, lo


def _log_decay(alow, wgu_ref, bg_ref):
    z = _dot(_bf(alow), wgu_ref[...]) + bg_ref[...]
    return _log_sigmoid(z) * (1.0 / GATE_TAU)


def _const_spec(shape, layer):
    nd = len(shape)
    return pl.BlockSpec((None,) + tuple(shape), lambda *_: (layer,) + (0,) * nd,
                        pipeline_mode=pl.Buffered(1))


CONV_FIRST = CONV_HIST - (CONV_WIDTH - 1)


def _conv_block(cb, xp_sc, cv_sc, cw_ref, cb_ref, tile):
    cols = slice(cb * LANES, (cb + 1) * LANES)
    for rb in range(tile // CONV_ROWS):
        r0 = rb * CONV_ROWS
        acc = jnp.broadcast_to(cb_ref[:, cols], (CONV_ROWS, LANES))
        for j in range(CONV_WIDTH):
            s = r0 + CONV_FIRST + j
            acc = acc + cw_ref[j:j + 1, cols] * xp_sc[cb, s:s + CONV_ROWS, :]
        cv_sc[cb, r0:r0 + CONV_ROWS, :] = acc


def _gla_tile(aqk_sc, v_sc, g_sc, wgu_ref, bg_ref, gnw_ref, st_sc, oab_sc, tile):
    q0 = C_Q - C_A
    k0 = C_K - C_A
    n_ch = tile // GLA_CHUNK
    span = min(tile, CUMSUM_ROWS)
    trow = lax.broadcasted_iota(jnp.int32, (span, span), 0)
    tcol = lax.broadcasted_iota(jnp.int32, (span, span), 1)
    same_block = jnp.bitwise_and(trow, -GLA_CHUNK) == jnp.bitwise_and(tcol, -GLA_CHUNK)
    tril = _bf(jnp.where(jnp.logical_and(same_block, tcol <= trow), 1.0, 0.0))
    srow = lax.broadcasted_iota(jnp.int32, (GLA_HEADS * GLA_CHUNK, GLA_CHUNK), 0)
    scol = lax.broadcasted_iota(jnp.int32, (GLA_HEADS * GLA_CHUNK, GLA_CHUNK), 1)
    causal = scol <= jnp.bitwise_and(srow, GLA_CHUNK - 1)
    lane = lax.broadcasted_iota(jnp.int32, (1, GLA_KEY_WIDTH), 1)
    head_masks = [jnp.logical_and(lane >= hh * GLA_DK, lane < (hh + 1) * GLA_DK)
                  for hh in range(GLA_HEADS)]
    blocks = [slice(c * GLA_CHUNK, (c + 1) * GLA_CHUNK) for c in range(n_ch)]

    def stack_heads(a):
        return _bf(jnp.concatenate([jnp.where(m, a, 0.0) for m in head_masks], axis=0))

    g = _log_decay(aqk_sc[:, 0:LANES], wgu_ref, bg_ref)
    ghi, gmid, glo = _split3(g)
    b = jnp.concatenate(
        [_dot(tril, ghi[r0:r0 + span]) + _dot(tril, gmid[r0:r0 + span]) + _dot(tril, glo[r0:r0 + span])
         for r0 in range(0, tile, span)], axis=0)
    last = [b[r.stop - 1:r.stop, :] for r in blocks]
    b_last = jnp.concatenate([jnp.broadcast_to(l, (GLA_CHUNK, GLA_KEY_WIDTH)) for l in last], axis=0)
    q = aqk_sc[:, q0:k0] * (GLA_DK ** -0.5)
    k = aqk_sc[:, k0:k0 + GLA_KEY_WIDTH]
    q_dec = q * jnp.exp(b)
    k_inv = _bf(k * jnp.exp(-b))
    k_rem = k * jnp.exp(b_last - b)

    q_stack = [stack_heads(q_dec[r]) for r in blocks]
    k_stack = [stack_heads(k_rem[r]) for r in blocks]
    scores = [jnp.where(causal, _dot_nt(q_stack[c], k_inv[blocks[c]]), 0.0) for c in range(n_ch)]
    v_blk = [v_sc[r, :] for r in blocks]
    v_heads = [[vb[:, hh * GLA_DV:(hh + 1) * GLA_DV] for hh in range(GLA_HEADS)] for vb in v_blk]
    kv = [_dot(_bf(jnp.concatenate(v_heads[c], axis=0).T), k_stack[c]) for c in range(n_ch)]
    o_intra = [jnp.concatenate(
        [_dot(_bf(scores[c][hh * GLA_CHUNK:(hh + 1) * GLA_CHUNK]), _bf(v_heads[c][hh]))
         for hh in range(GLA_HEADS)], axis=0) for c in range(n_ch)]

    st = st_sc[...]
    states = []
    for c in range(n_ch):
        states.append(_bf(st))
        st = jnp.exp(last[c]) * st + kv[c]
    st_sc[...] = st

    for c in range(n_ch):
        o = _dot_nt(q_stack[c], states[c]) + o_intra[c]
        o = o * lax.rsqrt(jnp.mean(o * o, axis=-1, keepdims=True) + EPS)
        for hh in range(GLA_HEADS):
            cols = slice(hh * GLA_DV, (hh + 1) * GLA_DV)
            oh = o[hh * GLA_CHUNK:(hh + 1) * GLA_CHUNK] * gnw_ref[:, cols]
            gate = g_sc[blocks[c], cols]
            oab_sc[blocks[c], cols] = _bf(oh * _silu(gate))


def _mixer_prompt_kernel(x_ref, n1_ref, win_ref, wgu_ref, bg_ref, gnw_ref, cw_ref, cb_ref,
                         lnw_ref, lnb_ref, wout_ref,
                         xo_ref, gla_ref, conv_ref,
                         aqk_sc, v_sc, g_sc, st_sc, xp_sc, oab_sc, h_sc, cv_sc, *, tile, n_tiles):
    i = pl.program_id(1)

    @pl.when(i == 0)
    def _():
        st_sc[...] = jnp.zeros_like(st_sc)
        xp_sc[:, 0:CONV_HIST, :] = jnp.zeros((CONV_BLOCKS, CONV_HIST, LANES), jnp.float32)

    x = x_ref[...]
    h_sc[...] = _bf(_rms(x, n1_ref[...]))

    def project(c0, c1):
        return _dot(h_sc[...], win_ref[:, c0:c1])

    def project_glu(cb):
        u = project(C_U + 2 * LANES * cb, C_U + 2 * LANES * (cb + 1))
        xp_sc[cb, CONV_HIST:CONV_HIST + tile, :] = u[:, :LANES] * jax.nn.sigmoid(u[:, LANES:])

    project_glu(0)
    project_glu(1)
    _conv_block(0, xp_sc, cv_sc, cw_ref, cb_ref, tile)
    project_glu(2)
    _conv_block(1, xp_sc, cv_sc, cw_ref, cb_ref, tile)
    project_glu(3)
    _conv_block(2, xp_sc, cv_sc, cw_ref, cb_ref, tile)
    aqk_sc[...] = project(C_A, C_V)
    _conv_block(3, xp_sc, cv_sc, cw_ref, cb_ref, tile)
    v_sc[...] = project(C_V, C_G)
    g_sc[...] = project(C_G, IN_PAD)

    cv = jnp.concatenate([cv_sc[cb] for cb in range(CONV_BLOCKS)], axis=1)
    mu = jnp.mean(cv, axis=-1, keepdims=True)
    var = jnp.mean(jnp.square(cv - mu), axis=-1, keepdims=True)
    y = (cv - mu) * lax.rsqrt(var + EPS) * lnw_ref[...] + lnb_ref[...]
    oab_sc[:, GLA_WIDTH:GLA_WIDTH + CONV_CH] = _bf(_silu(y))
    xp_sc[:, 0:CONV_HIST, :] = xp_sc[:, tile:tile + CONV_HIST, :]

    _gla_tile(aqk_sc, v_sc, g_sc, wgu_ref, bg_ref, gnw_ref, st_sc, oab_sc, tile)

    xo_ref[...] = x + _dot(oab_sc[...], wout_ref[...])

    @pl.when(i == n_tiles - 1)
    def _():
        gla_ref[...] = st_sc[...].T.reshape(GLA_HEADS, GLA_DK, GLA_DV)
        for cb in range(CONV_BLOCKS):
            conv_ref[:, cb * LANES:(cb + 1) * LANES] = xp_sc[cb, CONV_FIRST:CONV_HIST, :]


def _mixer_prompt(x, layer, p):
    bsz, t, _ = x.shape
    tile = PROMPT_TILE
    n_tiles = t // tile
    kern = functools.partial(_mixer_prompt_kernel, tile=tile, n_tiles=n_tiles)
    return pl.pallas_call(
        kern,
        grid=(bsz, n_tiles),
        in_specs=[
            pl.BlockSpec((None, tile, D_MODEL), lambda b, i: (b, i, 0)),
            _const_spec((1, D_MODEL), layer),
            _const_spec((D_MODEL, IN_PAD), layer),
            _const_spec((LANES, GLA_KEY_WIDTH), layer),
            _const_spec((1, GLA_KEY_WIDTH), layer),
            _const_spec((1, GLA_WIDTH), layer),
            _const_spec((CONV_WIDTH, CONV_CH), layer),
            _const_spec((1, CONV_CH), layer),
            _const_spec((1, CONV_CH), layer),
            _const_spec((1, CONV_CH), layer),
            _const_spec((D_MODEL, D_MODEL), layer),
        ],
        out_specs=[
            pl.BlockSpec((None, tile, D_MODEL), lambda b, i: (b, i, 0)),
            pl.BlockSpec((None, GLA_HEADS, GLA_DK, GLA_DV), lambda b, i: (b, 0, 0, 0)),
            pl.BlockSpec((None, CONV_WIDTH - 1, CONV_CH), lambda b, i: (b, 0, 0)),
        ],
        out_shape=[
            jax.ShapeDtypeStruct((bsz, t, D_MODEL), jnp.float32),
            jax.ShapeDtypeStruct((bsz, GLA_HEADS, GLA_DK, GLA_DV), jnp.float32),
            jax.ShapeDtypeStruct((bsz, CONV_WIDTH - 1, CONV_CH), jnp.float32),
        ],
        scratch_shapes=[
            pltpu.VMEM((tile, C_V - C_A), jnp.float32),
            pltpu.VMEM((tile, GLA_WIDTH), jnp.float32),
            pltpu.VMEM((tile, GLA_WIDTH), jnp.float32),
            pltpu.VMEM((GLA_DV, GLA_KEY_WIDTH), jnp.float32),
            pltpu.VMEM((CONV_BLOCKS, CONV_HIST + tile, LANES), jnp.float32),
            pltpu.VMEM((tile, D_MODEL), jnp.bfloat16),
            pltpu.VMEM((tile, D_MODEL), jnp.bfloat16),
            pltpu.VMEM((CONV_BLOCKS, tile, LANES), jnp.float32),
        ],
        compiler_params=pltpu.CompilerParams(
            dimension_semantics=("arbitrary", "arbitrary"), vmem_limit_bytes=VMEM_LIMIT),
        name="mixer_prompt",
    )(x, p["norm1_w"], p["w_in"], p["w_gate_up"], p["b_gate"], p["gla_norm_w"], p["conv_w"],
      p["conv_b"], p["conv_ln_w"], p["conv_ln_b"], p["w_out"])


FF_BLOCKS = D_FF // LANES


FF_GROUP = 2


def _ffn_gate(up_r, cb, fw_ref, fb_ref, tile):
    def conv(blk):
        cols = slice(blk * LANES, (blk + 1) * LANES)
        acc = jnp.broadcast_to(fb_ref[:, cols], (tile, LANES))
        for j in range(FFN_CONV_WIDTH):
            r = FFN_HIST - (FFN_CONV_WIDTH - 1) + j
            acc = acc + fw_ref[j:j + 1, cols] * up_r[blk, r:r + tile, :]
        return acc

    return _bf(_silu(conv(cb)) * conv(FF_BLOCKS + cb))


def _ffn_prompt_kernel(x_ref, n2_ref, wup_ref, fw_ref, fb_ref, wdown_ref, nf_ref,
                       xo_ref, ffn_ref, up_sc, h_sc, gated_sc, *, tile, n_tiles, final_norm):
    i = pl.program_id(1)

    @pl.when(i == 0)
    def _():
        up_sc[:, 0:FFN_HIST, :] = jnp.zeros((2 * FF_BLOCKS, FFN_HIST, LANES), jnp.float32)

    x = x_ref[...]
    h_sc[...] = _bf(_rms(x, n2_ref[...]))
    xo_ref[...] = x
    width = FF_GROUP * LANES
    n_groups = FF_BLOCKS // FF_GROUP

    def up_group(grp):
        for half in range(2):
            c0 = half * D_FF + grp * width
            up = _dot(h_sc[...], wup_ref[:, c0:c0 + width])
            for j in range(FF_GROUP):
                up_sc[half * FF_BLOCKS + grp * FF_GROUP + j, FFN_HIST:FFN_HIST + tile, :] = (
                    up[:, j * LANES:(j + 1) * LANES])

    def gate_group(grp):
        for j in range(FF_GROUP):
            cb = grp * FF_GROUP + j
            gated_sc[:, cb * LANES:(cb + 1) * LANES] = _ffn_gate(up_sc, cb, fw_ref, fb_ref, tile)

    def down_group(grp):
        rows = slice(grp * width, (grp + 1) * width)
        xo_ref[...] += _dot(gated_sc[:, rows], wdown_ref[rows, :])

    for step in range(n_groups + 2):
        if step < n_groups:
            up_group(step)
        if 1 <= step <= n_groups:
            gate_group(step - 1)
        if step >= 2:
            down_group(step - 2)
    if final_norm:
        xo_ref[...] = _rms(xo_ref[...], nf_ref[...])

    @pl.when(i == n_tiles - 1)
    def _():
        for cb in range(2 * FF_BLOCKS):
            ffn_ref[:, cb * LANES:(cb + 1) * LANES] = up_sc[
                cb, FFN_HIST + tile - (FFN_CONV_WIDTH - 1):FFN_HIST + tile, :]

    up_sc[:, 0:FFN_HIST, :] = up_sc[:, tile:tile + FFN_HIST, :]


def _ffn_prompt(x, layer, p, final_norm):
    bsz, t, _ = x.shape
    tile = PROMPT_TILE
    n_tiles = t // tile
    kern = functools.partial(_ffn_prompt_kernel, tile=tile, n_tiles=n_tiles, final_norm=final_norm)
    return pl.pallas_call(
        kern,
        grid=(bsz, n_tiles),
        in_specs=[
            pl.BlockSpec((None, tile, D_MODEL), lambda b, i: (b, i, 0)),
            _const_spec((1, D_MODEL), layer),
            _const_spec((D_MODEL, 2 * D_FF), layer),
            _const_spec((FFN_CONV_WIDTH, 2 * D_FF), layer),
            _const_spec((1, 2 * D_FF), layer),
            _const_spec((D_FF, D_MODEL), layer),
            pl.BlockSpec((1, D_MODEL), lambda b, i: (0, 0), pipeline_mode=pl.Buffered(1)),
        ],
        out_specs=[
            pl.BlockSpec((None, tile, D_MODEL), lambda b, i: (b, i, 0)),
            pl.BlockSpec((None, FFN_CONV_WIDTH - 1, 2 * D_FF), lambda b, i: (b, 0, 0)),
        ],
        out_shape=[
            jax.ShapeDtypeStruct((bsz, t, D_MODEL), jnp.float32),
            jax.ShapeDtypeStruct((bsz, FFN_CONV_WIDTH - 1, 2 * D_FF), jnp.float32),
        ],
        scratch_shapes=[
            pltpu.VMEM((2 * FF_BLOCKS, FFN_HIST + tile, LANES), jnp.float32),
            pltpu.VMEM((tile, D_MODEL), jnp.bfloat16),
            pltpu.VMEM((tile, D_FF), jnp.bfloat16),
        ],
        compiler_params=pltpu.CompilerParams(
            dimension_semantics=("arbitrary", "arbitrary"), vmem_limit_bytes=VMEM_LIMIT),
        name="ffn_prompt",
    )(x, p["norm2_w"], p["w_up"], p["ffn_conv_w"], p["ffn_conv_b"], p["w_down"], p["norm_f_w"])


def _proj_sample_kernel(x_ref, n1_ref, win_ref, wgu_ref, bg_ref, cs_ref, cw_ref, cb_ref,
                        lnw_ref, lnb_ref,
                        qd_ref, k_ref, a_ref, v_ref, sc_ref, g_ref, glu_ref, ob_ref):
    x = x_ref[...]
    proj = _dot(_bf(_rms(x, n1_ref[...])), win_ref[...])
    g = _log_decay(proj[:, C_A:C_A + LANES], wgu_ref, bg_ref)
    q = proj[:, C_Q:C_K] * (GLA_DK ** -0.5)
    k = proj[:, C_K:C_V]
    q_dec = q * jnp.exp(g)
    k_inv = k * jnp.exp(-g)
    qd_ref[...] = q_dec
    k_ref[...] = k * jnp.exp(g - g)
    a_ref[...] = jnp.exp(g)
    v_ref[...] = proj[:, C_V:C_G]
    g_ref[...] = proj[:, C_G:IN_PAD]
    qk = q_dec * k_inv
    for hh in range(GLA_HEADS):
        sc_ref[:, hh * LANES:(hh + 1) * LANES] = jnp.broadcast_to(
            jnp.sum(qk[:, hh * GLA_DK:(hh + 1) * GLA_DK], axis=-1, keepdims=True),
            (x.shape[0], LANES))
    glu = jnp.concatenate(
        [proj[:, C_U + 2 * LANES * cb:C_U + 2 * LANES * cb + LANES]
         * jax.nn.sigmoid(proj[:, C_U + 2 * LANES * cb + LANES:C_U + 2 * LANES * (cb + 1)])
         for cb in range(CONV_BLOCKS)], axis=1)
    glu_ref[...] = glu
    acc = cb_ref[...] + cw_ref[CONV_WIDTH - 1:CONV_WIDTH, :] * glu
    for j in range(CONV_WIDTH - 1):
        acc = acc + cw_ref[j:j + 1, :] * cs_ref[j]
    mu = jnp.mean(acc, axis=-1, keepdims=True)
    var = jnp.mean(jnp.square(acc - mu), axis=-1, keepdims=True)
    y = (acc - mu) * lax.rsqrt(var + EPS) * lnw_ref[...] + lnb_ref[...]
    ob_ref[...] = _silu(y)


def _proj_sample(x, cs_t, layer, p):
    n = x.shape[0]
    full = lambda shape: pl.BlockSpec(shape, lambda i: (0,) * len(shape))
    f32 = jnp.float32
    return pl.pallas_call(
        _proj_sample_kernel,
        grid=(1,),
        in_specs=[
            full((n, D_MODEL)),
            _const_spec((1, D_MODEL), layer),
            _const_spec((D_MODEL, IN_PAD), layer),
            _const_spec((LANES, GLA_KEY_WIDTH), layer),
            _const_spec((1, GLA_KEY_WIDTH), layer),
            full((CONV_WIDTH - 1, n, CONV_CH)),
            _const_spec((CONV_WIDTH, CONV_CH), layer),
            _const_spec((1, CONV_CH), layer),
            _const_spec((1, CONV_CH), layer),
            _const_spec((1, CONV_CH), layer),
        ],
        out_specs=[full((n, GLA_KEY_WIDTH)), full((n, GLA_KEY_WIDTH)), full((n, GLA_KEY_WIDTH)),
                   full((n, GLA_WIDTH)), full((n, GLA_WIDTH)), full((n, GLA_WIDTH)),
                   full((n, CONV_CH)), full((n, CONV_CH))],
        out_shape=[jax.ShapeDtypeStruct((n, GLA_KEY_WIDTH), f32)] * 3
        + [jax.ShapeDtypeStruct((n, GLA_WIDTH), f32)] * 3
        + [jax.ShapeDtypeStruct((n, CONV_CH), f32)] * 2,
        compiler_params=pltpu.CompilerParams(
            dimension_semantics=("arbitrary",), vmem_limit_bytes=VMEM_LIMIT),
        name="proj_sample",
    )(x, p["norm1_w"], p["w_in"], p["w_gate_up"], p["b_gate"], cs_t, p["conv_w"], p["conv_b"],
      p["conv_ln_w"], p["conv_ln_b"])


def _gla_sample_kernel(qd_ref, k_ref, a_ref, v_ref, s_ref, o_ref, so_ref):
    def column(ref, bb):
        return jnp.broadcast_to(ref[bb], (LANES, GLA_KEY_WIDTH)).T

    def body(bb, carry):
        qc = column(qd_ref, bb)
        kc = column(k_ref, bb)
        ac = column(a_ref, bb)
        s = s_ref[bb].reshape(GLA_HEADS * GLA_DK, GLA_DV)
        v = v_ref[bb]
        vb = jnp.concatenate(
            [jnp.broadcast_to(v[:, hh * GLA_DV:(hh + 1) * GLA_DV], (GLA_DK, GLA_DV))
             for hh in range(GLA_HEADS)], axis=0)
        qs = qc * s
        o_ref[bb] = jnp.concatenate(
            [jnp.sum(qs[hh * GLA_DK:(hh + 1) * GLA_DK], axis=0, keepdims=True)
             for hh in range(GLA_HEADS)], axis=1)
        so_ref[bb] = (ac * s + kc * vb).reshape(GLA_HEADS, GLA_DK, GLA_DV)
        return carry

    lax.fori_loop(0, SAMPLE_GLA_BLOCK, body, 0)


def _gla_sample(qd, k, a, v, s):
    n = qd.shape[0]
    blk = SAMPLE_GLA_BLOCK
    vec = lambda w: pl.BlockSpec((blk, 1, w), lambda i: (i, 0, 0))
    sspec = pl.BlockSpec((blk, GLA_HEADS, GLA_DK, GLA_DV), lambda i: (i, 0, 0, 0))
    o, s_new = pl.pallas_call(
        _gla_sample_kernel,
        grid=(n // blk,),
        in_specs=[vec(GLA_KEY_WIDTH), vec(GLA_KEY_WIDTH), vec(GLA_KEY_WIDTH), vec(GLA_WIDTH), sspec],
        out_specs=[vec(GLA_WIDTH), sspec],
        out_shape=[jax.ShapeDtypeStruct((n, 1, GLA_WIDTH), jnp.float32),
                   jax.ShapeDtypeStruct(s.shape, jnp.float32)],
        compiler_params=pltpu.CompilerParams(
            dimension_semantics=("arbitrary",), vmem_limit_bytes=VMEM_LIMIT),
        name="gla_sample",
    )(qd[:, None, :], k[:, None, :], a[:, None, :], v[:, None, :], s)
    return o[:, 0, :], s_new


def _tail_sample_kernel(x_ref, oi_ref, sc_ref, v_ref, g_ref, ob_ref, gnw_ref, wout_ref,
                        n2_ref, wup_ref, st0_ref, st1_ref, fw_ref, fb_ref, wdown_ref, nf_ref,
                        xo_ref, up_ref, *, final_norm):
    parts = []
    for hh in range(GLA_HEADS):
        cols = slice(hh * GLA_DV, (hh + 1) * GLA_DV)
        o = oi_ref[:, cols] + sc_ref[:, cols] * v_ref[:, cols]
        o = o * lax.rsqrt(jnp.mean(o * o, axis=-1, keepdims=True) + EPS)
        parts.append(_bf(o * gnw_ref[:, cols] * _silu(g_ref[:, cols])))
    parts.append(_bf(ob_ref[...]))
    x = x_ref[...] + _dot(jnp.concatenate(parts, axis=1), wout_ref[...])

    h2 = _bf(_rms(x, n2_ref[...]))
    up = _dot(h2, wup_ref[...])
    up_ref[...] = up
    upc = (fb_ref[...] + fw_ref[0:1, :] * st0_ref[...] + fw_ref[1:2, :] * st1_ref[...]
           + fw_ref[2:3, :] * up)
    gated = _bf(_silu(upc[:, :D_FF]) * upc[:, D_FF:])
    y = x + _dot(gated, wdown_ref[...])
    if final_norm:
        y = _rms(y, nf_ref[...])
    xo_ref[...] = y


def _tail_sample(x, oi, sc, v, g, ob, st0, st1, layer, p, final_norm):
    n = x.shape[0]
    full = lambda shape: pl.BlockSpec(shape, lambda i: (0,) * len(shape))
    return pl.pallas_call(
        functools.partial(_tail_sample_kernel, final_norm=final_norm),
        grid=(1,),
        in_specs=[
            full((n, D_MODEL)), full((n, GLA_WIDTH)), full((n, GLA_WIDTH)), full((n, GLA_WIDTH)),
            full((n, GLA_WIDTH)), full((n, CONV_CH)),
            _const_spec((1, GLA_WIDTH), layer),
            _const_spec((D_MODEL, D_MODEL), layer),
            _const_spec((1, D_MODEL), layer),
            _const_spec((D_MODEL, 2 * D_FF), layer),
            full((n, 2 * D_FF)), full((n, 2 * D_FF)),
            _const_spec((FFN_CONV_WIDTH, 2 * D_FF), layer),
            _const_spec((1, 2 * D_FF), layer),
            _const_spec((D_FF, D_MODEL), layer),
            pl.BlockSpec((1, D_MODEL), lambda i: (0, 0)),
        ],
        out_specs=[full((n, D_MODEL)), full((n, 2 * D_FF))],
        out_shape=[jax.ShapeDtypeStruct((n, D_MODEL), jnp.float32),
                   jax.ShapeDtypeStruct((n, 2 * D_FF), jnp.float32)],
        compiler_params=pltpu.CompilerParams(
            dimension_semantics=("arbitrary",), vmem_limit_bytes=VMEM_LIMIT),
        name="tail_sample",
    )(x, oi, sc, v, g, ob, p["gla_norm_w"], p["w_out"], p["norm2_w"], p["w_up"], st0, st1,
      p["ffn_conv_w"], p["ffn_conv_b"], p["w_down"], p["norm_f_w"])


def _prepare_params(norm1_w, w_in, w_gate_up, b_gate, gla_norm_w, conv_w, conv_b, conv_ln_w,
                    conv_ln_b, w_out, norm2_w, w_up, ffn_conv_w, ffn_conv_b, w_down, norm_f_w):
    depth = w_in.shape[0]
    a0 = 2 * GLA_KEY_WIDTH + 2 * GLA_WIDTH
    u0 = a0 + GATE_RANK
    pairs = []
    for cb in range(CONV_BLOCKS):
        pairs.append(w_in[:, :, u0 + cb * LANES:u0 + (cb + 1) * LANES])
        pairs.append(w_in[:, :, u0 + CONV_CH + cb * LANES:u0 + CONV_CH + (cb + 1) * LANES])
    w_in_r = jnp.concatenate(
        pairs + [w_in[:, :, a0:u0], jnp.zeros((depth, D_MODEL, LANES - GATE_RANK), w_in.dtype),
                 w_in[:, :, :a0]], axis=2)
    wgu = jnp.concatenate(
        [w_gate_up, jnp.zeros((depth, LANES - GATE_RANK, GLA_KEY_WIDTH), w_gate_up.dtype)], axis=1)
    row = lambda a: a[:, None, :]
    return {
        "norm1_w": row(norm1_w), "w_in": _bf(w_in_r), "w_gate_up": _bf(wgu), "b_gate": row(b_gate),
        "gla_norm_w": row(gla_norm_w), "conv_w": conv_w, "conv_b": row(conv_b),
        "conv_ln_w": row(conv_ln_w), "conv_ln_b": row(conv_ln_b), "w_out": _bf(w_out),
        "norm2_w": row(norm2_w), "w_up": _bf(w_up), "ffn_conv_w": ffn_conv_w,
        "ffn_conv_b": row(ffn_conv_b), "w_down": _bf(w_down), "norm_f_w": norm_f_w[None, :],
    }


def kernel(x_prompt, x_sample, state_gla, state_conv, state_ffn, norm1_w, w_in, w_gate_up, b_gate,
           gla_norm_w, conv_w, conv_b, conv_ln_w, conv_ln_b, w_out, norm2_w, w_up, ffn_conv_w,
           ffn_conv_b, w_down, norm_f_w):
    p = _prepare_params(norm1_w, w_in, w_gate_up, b_gate, gla_norm_w, conv_w, conv_b, conv_ln_w,
                        conv_ln_b, w_out, norm2_w, w_up, ffn_conv_w, ffn_conv_b, w_down, norm_f_w)
    depth = w_in.shape[0]

    x = x_prompt
    gla_p, conv_p, ffn_p = [], [], []
    for l in range(depth):
        x, s_new, c_buf = _mixer_prompt(x, l, p)
        x, f_buf = _ffn_prompt(x, l, p, final_norm=(l == depth - 1))
        gla_p.append(s_new)
        conv_p.append(c_buf)
        ffn_p.append(f_buf)
    y_prompt = x

    xs = x_sample[:, 0, :]
    conv_t = jnp.swapaxes(state_conv, 1, 2)
    ffn_t = jnp.swapaxes(state_ffn, 1, 2)
    gla_s, conv_s, ffn_s = [], [], []
    for l in range(depth):
        qd, k, a, v, sc, g, glu, ob = _proj_sample(xs, conv_t[l], l, p)
        oi, s_new = _gla_sample(qd, k, a, v, state_gla[l])
        xs, up = _tail_sample(xs, oi, sc, v, g, ob, ffn_t[l, 0], ffn_t[l, 1], l, p,
                              final_norm=(l == depth - 1))
        gla_s.append(s_new)
        conv_s.append(jnp.concatenate([state_conv[l][:, 1:], glu[:, None, :]], axis=1))
        ffn_s.append(jnp.stack([state_ffn[l][:, 1], up], axis=1))
    y_sample = xs[:, None, :]

    return (y_prompt, y_sample, jnp.stack(gla_p), jnp.stack(gla_s), jnp.stack(conv_p),
            jnp.stack(conv_s), jnp.stack(ffn_p), jnp.stack(ffn_s))
```

```python
import functools

import jax
import jax.numpy as jnp
from jax import lax
from jax.experimental import pallas as pl
from jax.experimental.pallas import tpu as pltpu

D_MODEL = 1024
GLA_WIDTH = 512
CONV_CH = 512
GLA_HEADS = 4
GLA_DV = 128
GLA_KEY_WIDTH = 256
GLA_DK = 64
GATE_RANK = 16
GATE_TAU = 16.0
GLA_CHUNK = 64
CONV_WIDTH = 31
FFN_CONV_WIDTH = 3
D_FF = 2816
EPS = 1e-6

LANES = 128
SUBLANES = 8

CONV_BLOCKS = CONV_CH // LANES
C_U = 0
C_A = C_U + 2 * CONV_CH
C_Q = C_A + LANES
C_K = C_Q + GLA_KEY_WIDTH
C_V = C_K + GLA_KEY_WIDTH
C_G = C_V + GLA_WIDTH
IN_PAD = C_G + GLA_WIDTH

CONV_HIST = 32
FFN_HIST = SUBLANES
MIXER_TILE = 512
MIXER_PASS = 512
FFN_TILE = 256
CUMSUM_ROWS = 256
CONV_ROWS = 64
SAMPLE_GLA_BLOCK = 16

VMEM_LIMIT = 56 * 1024 * 1024


def _rms(x, w):
    return x * lax.rsqrt(jnp.mean(x * x, axis=-1, keepdims=True) + EPS) * w


def _log_sigmoid(z):
    return jnp.minimum(z, 0.0) - jnp.log1p(jnp.exp(-jnp.abs(z)))


def _silu(x):
    return x * jax.nn.sigmoid(x)


def _bf(x):
    return x.astype(jnp.bfloat16)


def _dot(a, b):
    return jnp.dot(a, b, preferred_element_type=jnp.float32)


def _dot_nt(a, b):
    return lax.dot_general(a, b, (((1,), (1,)), ((), ())), preferred_element_type=jnp.float32)


def _split3(x):
    hi = _bf(x)
    r = x - hi.astype(jnp.float32)
    mid = _bf(r)
    lo = _bf(r - mid.astype(jnp.float32))
    return hi, mid, lo


def _log_decay(alow, wgu_ref, bg_ref):
    z = _dot(_bf(alow), wgu_ref[...]) + bg_ref[...]
    return _log_sigmoid(z) * (1.0 / GATE_TAU)


def _const_spec(shape, layer):
    nd = len(shape)
    return pl.BlockSpec((None,) + tuple(shape), lambda *_: (layer,) + (0,) * nd,
                        pipeline_mode=pl.Buffered(1))


CONV_FIRST = CONV_HIST - (CONV_WIDTH - 1)


def _conv_block(cb, xp_sc, cv_sc, cw_ref, cb_ref, base, rows):
    cols = slice(cb * LANES, (cb + 1) * LANES)
    for r0 in range(base, base + rows, CONV_ROWS):
        acc = jnp.broadcast_to(cb_ref[:, cols], (CONV_ROWS, LANES))
        for j in range(CONV_WIDTH):
            s = r0 + CONV_FIRST + j
            acc = acc + cw_ref[j:j + 1, cols] * xp_sc[cb, s:s + CONV_ROWS, :]
        cv_sc[cb, r0:r0 + CONV_ROWS, :] = acc


def _gla_rows(aqk_sc, v_sc, g_sc, wgu_ref, bg_ref, gnw_ref, st_sc, oab_sc, base, rows):
    q0 = C_Q - C_A
    k0 = C_K - C_A
    tile = rows
    n_ch = tile // GLA_CHUNK
    span = min(tile, CUMSUM_ROWS)
    win = slice(base, base + rows)
    trow = lax.broadcasted_iota(jnp.int32, (span, span), 0)
    tcol = lax.broadcasted_iota(jnp.int32, (span, span), 1)
    same_block = jnp.bitwise_and(trow, -GLA_CHUNK) == jnp.bitwise_and(tcol, -GLA_CHUNK)
    tril = _bf(jnp.where(jnp.logical_and(same_block, tcol <= trow), 1.0, 0.0))
    srow = lax.broadcasted_iota(jnp.int32, (GLA_HEADS * GLA_CHUNK, GLA_CHUNK), 0)
    scol = lax.broadcasted_iota(jnp.int32, (GLA_HEADS * GLA_CHUNK, GLA_CHUNK), 1)
    causal = scol <= jnp.bitwise_and(srow, GLA_CHUNK - 1)
    lane = lax.broadcasted_iota(jnp.int32, (1, GLA_KEY_WIDTH), 1)
    head_masks = [jnp.logical_and(lane >= hh * GLA_DK, lane < (hh + 1) * GLA_DK)
                  for hh in range(GLA_HEADS)]
    blocks = [slice(c * GLA_CHUNK, (c + 1) * GLA_CHUNK) for c in range(n_ch)]

    def stack_heads(a):
        return _bf(jnp.concatenate([jnp.where(m, a, 0.0) for m in head_masks], axis=0))

    g = _log_decay(aqk_sc[win, 0:LANES], wgu_ref, bg_ref)
    ghi, gmid, glo = _split3(g)
    b = jnp.concatenate(
        [_dot(tril, ghi[r0:r0 + span]) + _dot(tril, gmid[r0:r0 + span]) + _dot(tril, glo[r0:r0 + span])
         for r0 in range(0, tile, span)], axis=0)
    last = [b[r.stop - 1:r.stop, :] for r in blocks]
    b_last = jnp.concatenate([jnp.broadcast_to(l, (GLA_CHUNK, GLA_KEY_WIDTH)) for l in last], axis=0)
    q = aqk_sc[win, q0:k0] * (GLA_DK ** -0.5)
    k = aqk_sc[win, k0:k0 + GLA_KEY_WIDTH]
    q_dec = q * jnp.exp(b)
    k_inv = _bf(k * jnp.exp(-b))
    k_rem = k * jnp.exp(b_last - b)

    q_stack = [stack_heads(q_dec[r]) for r in blocks]
    k_stack = [stack_heads(k_rem[r]) for r in blocks]
    scores = [jnp.where(causal, _dot_nt(q_stack[c], k_inv[blocks[c]]), 0.0) for c in range(n_ch)]
    tile_rows = [slice(base + r.start, base + r.stop) for r in blocks]
    v_blk = [v_sc[r, :] for r in tile_rows]
    v_heads = [[vb[:, hh * GLA_DV:(hh + 1) * GLA_DV] for hh in range(GLA_HEADS)] for vb in v_blk]
    kv = [_dot(_bf(jnp.concatenate(v_heads[c], axis=0).T), k_stack[c]) for c in range(n_ch)]
    o_intra = [jnp.concatenate(
        [_dot(_bf(scores[c][hh * GLA_CHUNK:(hh + 1) * GLA_CHUNK]), _bf(v_heads[c][hh]))
         for hh in range(GLA_HEADS)], axis=0) for c in range(n_ch)]

    st = st_sc[...]
    states = []
    for c in range(n_ch):
        states.append(_bf(st))
        st = jnp.exp(last[c]) * st + kv[c]
    st_sc[...] = st

    for c in range(n_ch):
        o = _dot_nt(q_stack[c], states[c]) + o_intra[c]
        o = o * lax.rsqrt(jnp.mean(o * o, axis=-1, keepdims=True) + EPS)
        for hh in range(GLA_HEADS):
            cols = slice(hh * GLA_DV, (hh + 1) * GLA_DV)
            oh = o[hh * GLA_CHUNK:(hh + 1) * GLA_CHUNK] * gnw_ref[:, cols]
            gate = g_sc[tile_rows[c], cols]
            oab_sc[tile_rows[c], cols] = _bf(oh * _silu(gate))


def _mixer_prompt_kernel(x_ref, n1_ref, win_ref, wgu_ref, bg_ref, gnw_ref, cw_ref, cb_ref,
                         lnw_ref, lnb_ref, wout_ref,
                         xo_ref, gla_ref, conv_ref,
                         aqk_sc, v_sc, g_sc, st_sc, xp_sc, oab_sc, h_sc, cv_sc, *, tile, n_tiles):
    i = pl.program_id(1)

    @pl.when(i == 0)
    def _():
        st_sc[...] = jnp.zeros_like(st_sc)
        xp_sc[:, 0:CONV_HIST, :] = jnp.zeros((CONV_BLOCKS, CONV_HIST, LANES), jnp.float32)

    h_sc[...] = _bf(_rms(x_ref[...], n1_ref[...]))
    rows = min(tile, MIXER_PASS)
    passes = list(range(0, tile, rows))

    def project(base, c0, c1):
        return _dot(h_sc[base:base + rows, :], win_ref[:, c0:c1])

    def project_glu(base):
        u = project(base, C_U, C_A)
        for cb in range(CONV_BLOCKS):
            c0 = cb * LANES
            xp_sc[cb, CONV_HIST + base:CONV_HIST + base + rows, :] = (
                u[:, c0:c0 + LANES] * jax.nn.sigmoid(u[:, CONV_CH + c0:CONV_CH + c0 + LANES]))

    def project_aqk(base):
        aqk_sc[base:base + rows, :] = project(base, C_A, C_V)

    def project_v(base):
        v_sc[base:base + rows, :] = project(base, C_V, C_G)

    def project_g(base):
        g_sc[base:base + rows, :] = project(base, C_G, IN_PAD)

    def conv(cb, base):
        _conv_block(cb, xp_sc, cv_sc, cw_ref, cb_ref, base, rows)

    def layer_norm_swish(base):
        win = slice(base, base + rows)
        cv = jnp.concatenate([cv_sc[cb, win, :] for cb in range(CONV_BLOCKS)], axis=1)
        mu = jnp.mean(cv, axis=-1, keepdims=True)
        var = jnp.mean(jnp.square(cv - mu), axis=-1, keepdims=True)
        y = (cv - mu) * lax.rsqrt(var + EPS) * lnw_ref[...] + lnb_ref[...]
        oab_sc[win, GLA_WIDTH:GLA_WIDTH + CONV_CH] = _bf(_silu(y))

    def finish(base):
        win = slice(base, base + rows)
        xo_ref[win, :] = x_ref[win, :] + _dot(oab_sc[win, :], wout_ref[...])

    for stage in (project_glu, project_aqk, project_v, project_g):
        stage(passes[0])
    for idx, base in enumerate(passes):
        ahead = passes[idx + 1] if idx + 1 < len(passes) else None
        if ahead is not None:
            project_glu(ahead)
        conv(0, base)
        conv(1, base)
        if ahead is not None:
            project_aqk(ahead)
        conv(2, base)
        conv(3, base)
        if ahead is not None:
            project_v(ahead)
        layer_norm_swish(base)
        _gla_rows(aqk_sc, v_sc, g_sc, wgu_ref, bg_ref, gnw_ref, st_sc, oab_sc, base, rows)
        if ahead is not None:
            project_g(ahead)
        finish(base)
    xp_sc[:, 0:CONV_HIST, :] = xp_sc[:, tile:tile + CONV_HIST, :]

    @pl.when(i == n_tiles - 1)
    def _():
        gla_ref[...] = st_sc[...].T.reshape(GLA_HEADS, GLA_DK, GLA_DV)
        for cb in range(CONV_BLOCKS):
            conv_ref[:, cb * LANES:(cb + 1) * LANES] = xp_sc[cb, CONV_FIRST:CONV_HIST, :]


def _mixer_prompt(x, layer, p):
    bsz, t, _ = x.shape
    tile = MIXER_TILE
    n_tiles = t // tile
    kern = functools.partial(_mixer_prompt_kernel, tile=tile, n_tiles=n_tiles)
    return pl.pallas_call(
        kern,
        grid=(bsz, n_tiles),
        in_specs=[
            pl.BlockSpec((None, tile, D_MODEL), lambda b, i: (b, i, 0)),
            _const_spec((1, D_MODEL), layer),
            _const_spec((D_MODEL, IN_PAD), layer),
            _const_spec((LANES, GLA_KEY_WIDTH), layer),
            _const_spec((1, GLA_KEY_WIDTH), layer),
            _const_spec((1, GLA_WIDTH), layer),
            _const_spec((CONV_WIDTH, CONV_CH), layer),
            _const_spec((1, CONV_CH), layer),
            _const_spec((1, CONV_CH), layer),
            _const_spec((1, CONV_CH), layer),
            _const_spec((D_MODEL, D_MODEL), layer),
        ],
        out_specs=[
            pl.BlockSpec((None, tile, D_MODEL), lambda b, i: (b, i, 0)),
            pl.BlockSpec((None, GLA_HEADS, GLA_DK, GLA_DV), lambda b, i: (b, 0, 0, 0)),
            pl.BlockSpec((None, CONV_WIDTH - 1, CONV_CH), lambda b, i: (b, 0, 0)),
        ],
        out_shape=[
            jax.ShapeDtypeStruct((bsz, t, D_MODEL), jnp.float32),
            jax.ShapeDtypeStruct((bsz, GLA_HEADS, GLA_DK, GLA_DV), jnp.float32),
            jax.ShapeDtypeStruct((bsz, CONV_WIDTH - 1, CONV_CH), jnp.float32),
        ],
        scratch_shapes=[
            pltpu.VMEM((tile, C_V - C_A), jnp.float32),
            pltpu.VMEM((tile, GLA_WIDTH), jnp.float32),
            pltpu.VMEM((tile, GLA_WIDTH), jnp.float32),
            pltpu.VMEM((GLA_DV, GLA_KEY_WIDTH), jnp.float32),
            pltpu.VMEM((CONV_BLOCKS, CONV_HIST + tile, LANES), jnp.float32),
            pltpu.VMEM((tile, D_MODEL), jnp.bfloat16),
            pltpu.VMEM((tile, D_MODEL), jnp.bfloat16),
            pltpu.VMEM((CONV_BLOCKS, tile, LANES), jnp.float32),
        ],
        compiler_params=pltpu.CompilerParams(
            dimension_semantics=("arbitrary", "arbitrary"), vmem_limit_bytes=VMEM_LIMIT),
        name="mixer_prompt",
    )(x, p["norm1_w"], p["w_in"], p["w_gate_up"], p["b_gate"], p["gla_norm_w"], p["conv_w"],
      p["conv_b"], p["conv_ln_w"], p["conv_ln_b"], p["w_out"])


FF_BLOCKS = D_FF // LANES


FF_GROUP = 2


def _ffn_gate(up_r, cb, fw_ref, fb_ref, tile):
    def conv(blk):
        cols = slice(blk * LANES, (blk + 1) * LANES)
        acc = jnp.broadcast_to(fb_ref[:, cols], (tile, LANES))
        for j in range(FFN_CONV_WIDTH):
            r = FFN_HIST - (FFN_CONV_WIDTH - 1) + j
            acc = acc + fw_ref[j:j + 1, cols] * up_r[blk, r:r + tile, :]
        return acc

    return _bf(_silu(conv(cb)) * conv(FF_BLOCKS + cb))


def _ffn_prompt_kernel(x_ref, n2_ref, wup_ref, fw_ref, fb_ref, wdown_ref, nf_ref,
                       xo_ref, ffn_ref, up_sc, h_sc, gated_sc, *, tile, n_tiles, final_norm):
    i = pl.program_id(1)

    @pl.when(i == 0)
    def _():
        up_sc[:, 0:FFN_HIST, :] = jnp.zeros((2 * FF_BLOCKS, FFN_HIST, LANES), jnp.float32)

    x = x_ref[...]
    h_sc[...] = _bf(_rms(x, n2_ref[...]))
    xo_ref[...] = x
    width = FF_GROUP * LANES
    n_groups = FF_BLOCKS // FF_GROUP

    def up_group(grp):
        for half in range(2):
            c0 = half * D_FF + grp * width
            up = _dot(h_sc[...], wup_ref[:, c0:c0 + width])
            for j in range(FF_GROUP):
                up_sc[half * FF_BLOCKS + grp * FF_GROUP + j, FFN_HIST:FFN_HIST + tile, :] = (
                    up[:, j * LANES:(j + 1) * LANES])

    def gate_group(grp):
        for j in range(FF_GROUP):
            cb = grp * FF_GROUP + j
            gated_sc[:, cb * LANES:(cb + 1) * LANES] = _ffn_gate(up_sc, cb, fw_ref, fb_ref, tile)

    def down_group(grp):
        rows = slice(grp * width, (grp + 1) * width)
        xo_ref[...] += _dot(gated_sc[:, rows], wdown_ref[rows, :])

    for step in range(n_groups + 2):
        if step < n_groups:
            up_group(step)
        if 1 <= step <= n_groups:
            gate_group(step - 1)
        if step >= 2:
            down_group(step - 2)
    if final_norm:
        xo_ref[...] = _rms(xo_ref[...], nf_ref[...])

    @pl.when(i == n_tiles - 1)
    def _():
        for cb in range(2 * FF_BLOCKS):
            ffn_ref[:, cb * LANES:(cb + 1) * LANES] = up_sc[
                cb, FFN_HIST + tile - (FFN_CONV_WIDTH - 1):FFN_HIST + tile, :]

    up_sc[:, 0:FFN_HIST, :] = up_sc[:, tile:tile + FFN_HIST, :]


def _ffn_prompt(x, layer, p, final_norm):
    bsz, t, _ = x.shape
    tile = FFN_TILE
    n_tiles = t // tile
    kern = functools.partial(_ffn_prompt_kernel, tile=tile, n_tiles=n_tiles, final_norm=final_norm)
    return pl.pallas_call(
        kern,
        grid=(bsz, n_tiles),
        in_specs=[
            pl.BlockSpec((None, tile, D_MODEL), lambda b, i: (b, i, 0)),
            _const_spec((1, D_MODEL), layer),
            _const_spec((D_MODEL, 2 * D_FF), layer),
            _const_spec((FFN_CONV_WIDTH, 2 * D_FF), layer),
            _const_spec((1, 2 * D_FF), layer),
            _const_spec((D_FF, D_MODEL), layer),
            pl.BlockSpec((1, D_MODEL), lambda b, i: (0, 0), pipeline_mode=pl.Buffered(1)),
        ],
        out_specs=[
            pl.BlockSpec((None, tile, D_MODEL), lambda b, i: (b, i, 0)),
            pl.BlockSpec((None, FFN_CONV_WIDTH - 1, 2 * D_FF), lambda b, i: (b, 0, 0)),
        ],
        out_shape=[
            jax.ShapeDtypeStruct((bsz, t, D_MODEL), jnp.float32),
            jax.ShapeDtypeStruct((bsz, FFN_CONV_WIDTH - 1, 2 * D_FF), jnp.float32),
        ],
        scratch_shapes=[
            pltpu.VMEM((2 * FF_BLOCKS, FFN_HIST + tile, LANES), jnp.float32),
            pltpu.VMEM((tile, D_MODEL), jnp.bfloat16),
            pltpu.VMEM((tile, D_FF), jnp.bfloat16),
        ],
        compiler_params=pltpu.CompilerParams(
            dimension_semantics=("arbitrary", "arbitrary"), vmem_limit_bytes=VMEM_LIMIT),
        name="ffn_prompt",
    )(x, p["norm2_w"], p["w_up"], p["ffn_conv_w"], p["ffn_conv_b"], p["w_down"], p["norm_f_w"])


def _proj_sample_kernel(x_ref, n1_ref, win_ref, wgu_ref, bg_ref, cs_ref, cw_ref, cb_ref,
                        lnw_ref, lnb_ref,
                        qd_ref, k_ref, a_ref, v_ref, sc_ref, g_ref, glu_ref, ob_ref):
    x = x_ref[...]
    proj = _dot(_bf(_rms(x, n1_ref[...])), win_ref[...])
    g = _log_decay(proj[:, C_A:C_A + LANES], wgu_ref, bg_ref)
    q = proj[:, C_Q:C_K] * (GLA_DK ** -0.5)
    k = proj[:, C_K:C_V]
    q_dec = q * jnp.exp(g)
    k_inv = k * jnp.exp(-g)
    qd_ref[...] = q_dec
    k_ref[...] = k * jnp.exp(g - g)
    a_ref[...] = jnp.exp(g)
    v_ref[...] = proj[:, C_V:C_G]
    g_ref[...] = proj[:, C_G:IN_PAD]
    qk = q_dec * k_inv
    for hh in range(GLA_HEADS):
        sc_ref[:, hh * LANES:(hh + 1) * LANES] = jnp.broadcast_to(
            jnp.sum(qk[:, hh * GLA_DK:(hh + 1) * GLA_DK], axis=-1, keepdims=True),
            (x.shape[0], LANES))
    glu = proj[:, C_U:C_U + CONV_CH] * jax.nn.sigmoid(proj[:, C_U + CONV_CH:C_A])
    glu_ref[...] = glu
    acc = cb_ref[...] + cw_ref[CONV_WIDTH - 1:CONV_WIDTH, :] * glu
    for j in range(CONV_WIDTH - 1):
        acc = acc + cw_ref[j:j + 1, :] * cs_ref[j]
    mu = jnp.mean(acc, axis=-1, keepdims=True)
    var = jnp.mean(jnp.square(acc - mu), axis=-1, keepdims=True)
    y = (acc - mu) * lax.rsqrt(var + EPS) * lnw_ref[...] + lnb_ref[...]
    ob_ref[...] = _silu(y)


def _proj_sample(x, cs_t, layer, p):
    n = x.shape[0]
    full = lambda shape: pl.BlockSpec(shape, lambda i: (0,) * len(shape))
    f32 = jnp.float32
    return pl.pallas_call(
        _proj_sample_kernel,
        grid=(1,),
        in_specs=[
            full((n, D_MODEL)),
            _const_spec((1, D_MODEL), layer),
            _const_spec((D_MODEL, IN_PAD), layer),
            _const_spec((LANES, GLA_KEY_WIDTH), layer),
            _const_spec((1, GLA_KEY_WIDTH), layer),
            pl.BlockSpec((None, CONV_WIDTH - 1, n, CONV_CH), lambda i: (layer, 0, 0, 0)),
            _const_spec((CONV_WIDTH, CONV_CH), layer),
            _const_spec((1, CONV_CH), layer),
            _const_spec((1, CONV_CH), layer),
            _const_spec((1, CONV_CH), layer),
        ],
        out_specs=[full((n, GLA_KEY_WIDTH)), full((n, GLA_KEY_WIDTH)), full((n, GLA_KEY_WIDTH)),
                   full((n, GLA_WIDTH)), full((n, GLA_WIDTH)), full((n, GLA_WIDTH)),
                   full((n, CONV_CH)), full((n, CONV_CH))],
        out_shape=[jax.ShapeDtypeStruct((n, GLA_KEY_WIDTH), f32)] * 3
        + [jax.ShapeDtypeStruct((n, GLA_WIDTH), f32)] * 3
        + [jax.ShapeDtypeStruct((n, CONV_CH), f32)] * 2,
        compiler_params=pltpu.CompilerParams(
            dimension_semantics=("arbitrary",), vmem_limit_bytes=VMEM_LIMIT),
        name="proj_sample",
    )(x, p["norm1_w"], p["w_in"], p["w_gate_up"], p["b_gate"], cs_t, p["conv_w"], p["conv_b"],
      p["conv_ln_w"], p["conv_ln_b"])


def _gla_sample_kernel(qd_ref, k_ref, a_ref, v_ref, s_ref, o_ref, so_ref):
    def column(ref, bb):
        return jnp.broadcast_to(ref[bb], (LANES, GLA_KEY_WIDTH)).T

    def body(bb, carry):
        qc = column(qd_ref, bb)
        kc = column(k_ref, bb)
        ac = column(a_ref, bb)
        s = s_ref[bb].reshape(GLA_HEADS * GLA_DK, GLA_DV)
        v = v_ref[bb]
        vb = jnp.concatenate(
            [jnp.broadcast_to(v[:, hh * GLA_DV:(hh + 1) * GLA_DV], (GLA_DK, GLA_DV))
             for hh in range(GLA_HEADS)], axis=0)
        qs = qc * s
        o_ref[bb] = jnp.concatenate(
            [jnp.sum(qs[hh * GLA_DK:(hh + 1) * GLA_DK], axis=0, keepdims=True)
             for hh in range(GLA_HEADS)], axis=1)
        so_ref[bb] = (ac * s + kc * vb).reshape(GLA_HEADS, GLA_DK, GLA_DV)
        return carry

    lax.fori_loop(0, SAMPLE_GLA_BLOCK, body, 0)


def _gla_sample(qd, k, a, v, s_all, layer):
    n = qd.shape[0]
    blk = SAMPLE_GLA_BLOCK
    vec = lambda w: pl.BlockSpec((blk, 1, w), lambda i: (i, 0, 0))
    sspec = pl.BlockSpec((None, blk, GLA_HEADS, GLA_DK, GLA_DV), lambda i: (layer, i, 0, 0, 0))
    o, s_new = pl.pallas_call(
        _gla_sample_kernel,
        grid=(n // blk,),
        in_specs=[vec(GLA_KEY_WIDTH), vec(GLA_KEY_WIDTH), vec(GLA_KEY_WIDTH), vec(GLA_WIDTH), sspec],
        out_specs=[vec(GLA_WIDTH), sspec],
        out_shape=[jax.ShapeDtypeStruct((n, 1, GLA_WIDTH), jnp.float32),
                   jax.ShapeDtypeStruct(s_all.shape, jnp.float32)],
        input_output_aliases={4: 1},
        compiler_params=pltpu.CompilerParams(
            dimension_semantics=("arbitrary",), vmem_limit_bytes=VMEM_LIMIT),
        name="gla_sample",
    )(qd[:, None, :], k[:, None, :], a[:, None, :], v[:, None, :], s_all)
    return o[:, 0, :], s_new


def _tail_sample_kernel(x_ref, oi_ref, sc_ref, v_ref, g_ref, ob_ref, gnw_ref, wout_ref,
                        n2_ref, wup_ref, st0_ref, st1_ref, fw_ref, fb_ref, wdown_ref, nf_ref,
                        xo_ref, up_ref, *, final_norm):
    parts = []
    for hh in range(GLA_HEADS):
        cols = slice(hh * GLA_DV, (hh + 1) * GLA_DV)
        o = oi_ref[:, cols] + sc_ref[:, cols] * v_ref[:, cols]
        o = o * lax.rsqrt(jnp.mean(o * o, axis=-1, keepdims=True) + EPS)
        parts.append(_bf(o * gnw_ref[:, cols] * _silu(g_ref[:, cols])))
    parts.append(_bf(ob_ref[...]))
    x = x_ref[...] + _dot(jnp.concatenate(parts, axis=1), wout_ref[...])

    h2 = _bf(_rms(x, n2_ref[...]))
    up = _dot(h2, wup_ref[...])
    up_ref[...] = up
    upc = (fb_ref[...] + fw_ref[0:1, :] * st0_ref[...] + fw_ref[1:2, :] * st1_ref[...]
           + fw_ref[2:3, :] * up)
    gated = _bf(_silu(upc[:, :D_FF]) * upc[:, D_FF:])
    y = x + _dot(gated, wdown_ref[...])
    if final_norm:
        y = _rms(y, nf_ref[...])
    xo_ref[...] = y


def _tail_sample(x, oi, sc, v, g, ob, ffn_t, layer, p, final_norm):
    n = x.shape[0]
    full = lambda shape: pl.BlockSpec(shape, lambda i: (0,) * len(shape))
    plane = lambda j: pl.BlockSpec((None, None, n, 2 * D_FF), lambda i: (layer, j, 0, 0))
    return pl.pallas_call(
        functools.partial(_tail_sample_kernel, final_norm=final_norm),
        grid=(1,),
        in_specs=[
            full((n, D_MODEL)), full((n, GLA_WIDTH)), full((n, GLA_WIDTH)), full((n, GLA_WIDTH)),
            full((n, GLA_WIDTH)), full((n, CONV_CH)),
            _const_spec((1, GLA_WIDTH), layer),
            _const_spec((D_MODEL, D_MODEL), layer),
            _const_spec((1, D_MODEL), layer),
            _const_spec((D_MODEL, 2 * D_FF), layer),
            plane(0), plane(1),
            _const_spec((FFN_CONV_WIDTH, 2 * D_FF), layer),
            _const_spec((1, 2 * D_FF), layer),
            _const_spec((D_FF, D_MODEL), layer),
            pl.BlockSpec((1, D_MODEL), lambda i: (0, 0)),
        ],
        out_specs=[full((n, D_MODEL)), full((n, 2 * D_FF))],
        out_shape=[jax.ShapeDtypeStruct((n, D_MODEL), jnp.float32),
                   jax.ShapeDtypeStruct((n, 2 * D_FF), jnp.float32)],
        compiler_params=pltpu.CompilerParams(
            dimension_semantics=("arbitrary",), vmem_limit_bytes=VMEM_LIMIT),
        name="tail_sample",
    )(x, oi, sc, v, g, ob, p["gla_norm_w"], p["w_out"], p["norm2_w"], p["w_up"], ffn_t, ffn_t,
      p["ffn_conv_w"], p["ffn_conv_b"], p["w_down"], p["norm_f_w"])


def _prepare_params(norm1_w, w_in, w_gate_up, b_gate, gla_norm_w, conv_w, conv_b, conv_ln_w,
                    conv_ln_b, w_out, norm2_w, w_up, ffn_conv_w, ffn_conv_b, w_down, norm_f_w):
    depth = w_in.shape[0]
    a0 = 2 * GLA_KEY_WIDTH + 2 * GLA_WIDTH
    u0 = a0 + GATE_RANK
    w_in_r = jnp.concatenate(
        [w_in[:, :, u0:], w_in[:, :, a0:u0],
         jnp.zeros((depth, D_MODEL, LANES - GATE_RANK), w_in.dtype), w_in[:, :, :a0]], axis=2)
    wgu = jnp.concatenate(
        [w_gate_up, jnp.zeros((depth, LANES - GATE_RANK, GLA_KEY_WIDTH), w_gate_up.dtype)], axis=1)
    row = lambda a: a[:, None, :]
    return {
        "norm1_w": row(norm1_w), "w_in": _bf(w_in_r), "w_gate_up": _bf(wgu), "b_gate": row(b_gate),
        "gla_norm_w": row(gla_norm_w), "conv_w": conv_w, "conv_b": row(conv_b),
        "conv_ln_w": row(conv_ln_w), "conv_ln_b": row(conv_ln_b), "w_out": _bf(w_out),
        "norm2_w": row(norm2_w), "w_up": _bf(w_up), "ffn_conv_w": ffn_conv_w,
        "ffn_conv_b": row(ffn_conv_b), "w_down": _bf(w_down), "norm_f_w": norm_f_w[None, :],
    }


def kernel(x_prompt, x_sample, state_gla, state_conv, state_ffn, norm1_w, w_in, w_gate_up, b_gate,
           gla_norm_w, conv_w, conv_b, conv_ln_w, conv_ln_b, w_out, norm2_w, w_up, ffn_conv_w,
           ffn_conv_b, w_down, norm_f_w):
    p = _prepare_params(norm1_w, w_in, w_gate_up, b_gate, gla_norm_w, conv_w, conv_b, conv_ln_w,
                        conv_ln_b, w_out, norm2_w, w_up, ffn_conv_w, ffn_conv_b, w_down, norm_f_w)
    depth = w_in.shape[0]

    x = x_prompt
    gla_p, conv_p, ffn_p = [], [], []
    for l in range(depth):
        x, s_new, c_buf = _mixer_prompt(x, l, p)
        x, f_buf = _ffn_prompt(x, l, p, final_norm=(l == depth - 1))
        gla_p.append(s_new)
        conv_p.append(c_buf)
        ffn_p.append(f_buf)
    y_prompt = x

    xs = x_sample[:, 0, :]
    conv_t = jnp.swapaxes(state_conv, 1, 2)
    ffn_t = jnp.swapaxes(state_ffn, 1, 2)
    gla_s = state_gla
    glus, ups = [], []
    for l in range(depth):
        qd, k, a, v, sc, g, glu, ob = _proj_sample(xs, conv_t, l, p)
        oi, gla_s = _gla_sample(qd, k, a, v, gla_s, l)
        xs, up = _tail_sample(xs, oi, sc, v, g, ob, ffn_t, l, p, final_norm=(l == depth - 1))
        glus.append(glu)
        ups.append(up)
    y_sample = xs[:, None, :]
    conv_s = jnp.concatenate([state_conv[:, :, 1:, :], jnp.stack(glus)[:, :, None, :]], axis=2)
    ffn_s = jnp.concatenate([state_ffn[:, :, 1:, :], jnp.stack(ups)[:, :, None, :]], axis=2)

    return (y_prompt, y_sample, jnp.stack(gla_p), gla_s, jnp.stack(conv_p), conv_s,
            jnp.stack(ffn_p), ffn_s)
```

```python
import functools

import jax
import jax.numpy as jnp
from jax import lax
from jax.experimental import pallas as pl
from jax.experimental.pallas import tpu as pltpu

D_MODEL = 1024
GLA_WIDTH = 512
CONV_CH = 512
GLA_HEADS = 4
GLA_DV = 128
GLA_KEY_WIDTH = 256
GLA_DK = 64
GATE_RANK = 16
GATE_TAU = 16.0
GLA_CHUNK = 64
CONV_WIDTH = 31
FFN_CONV_WIDTH = 3
D_FF = 2816
EPS = 1e-6

LANES = 128
SUBLANES = 8

CONV_BLOCKS = CONV_CH // LANES
C_U = 0
C_A = C_U + 2 * CONV_CH
C_Q = C_A + LANES
C_K = C_Q + GLA_KEY_WIDTH
C_V = C_K + GLA_KEY_WIDTH
C_G = C_V + GLA_WIDTH
IN_PAD = C_G + GLA_WIDTH

CONV_HIST = 32
FFN_HIST = SUBLANES
MIXER_TILE = 512
MIXER_PASS = 512
FFN_TILE = 256
CUMSUM_ROWS = 256
CONV_ROWS = 64
SAMPLE_GLA_BLOCK = 16
SAMPLE_GLA_UNROLL = 4

VMEM_LIMIT = 56 * 1024 * 1024


def _rms(x, w):
    return x * lax.rsqrt(jnp.mean(x * x, axis=-1, keepdims=True) + EPS) * w


def _log_sigmoid(z):
    return jnp.minimum(z, 0.0) - jnp.log1p(jnp.exp(-jnp.abs(z)))


def _silu(x):
    return x * jax.nn.sigmoid(x)


def _bf(x):
    return x.astype(jnp.bfloat16)


def _dot(a, b):
    return jnp.dot(a, b, preferred_element_type=jnp.float32)


def _dot_nt(a, b):
    return lax.dot_general(a, b, (((1,), (1,)), ((), ())), preferred_element_type=jnp.float32)


def _split3(x):
    hi = _bf(x)
    r = x - hi.astype(jnp.float32)
    mid = _bf(r)
    lo = _bf(r - mid.astype(jnp.float32))
    return hi, mid, lo


def _log_decay(alow, wgu_ref, bg_ref):
    z = _dot(_bf(alow), wgu_ref[...]) + bg_ref[...]
    return _log_sigmoid(z) * (1.0 / GATE_TAU)


def _const_spec(shape, layer):
    nd = len(shape)
    return pl.BlockSpec((None,) + tuple(shape), lambda *_: (layer,) + (0,) * nd,
                        pipeline_mode=pl.Buffered(1))


CONV_FIRST = CONV_HIST - (CONV_WIDTH - 1)


def _conv_block(cb, xp_sc, cv_sc, cw_ref, cb_ref, base, rows):
    cols = slice(cb * LANES, (cb + 1) * LANES)
    for r0 in range(base, base + rows, CONV_ROWS):
        acc = jnp.broadcast_to(cb_ref[:, cols], (CONV_ROWS, LANES))
        for j in range(CONV_WIDTH):
            s = r0 + CONV_FIRST + j
            acc = acc + cw_ref[j:j + 1, cols] * xp_sc[cb, s:s + CONV_ROWS, :]
        cv_sc[cb, r0:r0 + CONV_ROWS, :] = acc


def _gla_rows(aqk_sc, v_sc, g_sc, wgu_ref, bg_ref, gnw_ref, st_sc, oab_sc, base, rows):
    q0 = C_Q - C_A
    k0 = C_K - C_A
    tile = rows
    n_ch = tile // GLA_CHUNK
    span = min(tile, CUMSUM_ROWS)
    win = slice(base, base + rows)
    trow = lax.broadcasted_iota(jnp.int32, (span, span), 0)
    tcol = lax.broadcasted_iota(jnp.int32, (span, span), 1)
    same_block = jnp.bitwise_and(trow, -GLA_CHUNK) == jnp.bitwise_and(tcol, -GLA_CHUNK)
    tril = _bf(jnp.where(jnp.logical_and(same_block, tcol <= trow), 1.0, 0.0))
    srow = lax.broadcasted_iota(jnp.int32, (GLA_HEADS * GLA_CHUNK, GLA_CHUNK), 0)
    scol = lax.broadcasted_iota(jnp.int32, (GLA_HEADS * GLA_CHUNK, GLA_CHUNK), 1)
    causal = scol <= jnp.bitwise_and(srow, GLA_CHUNK - 1)
    lane = lax.broadcasted_iota(jnp.int32, (1, GLA_KEY_WIDTH), 1)
    head_masks = [jnp.logical_and(lane >= hh * GLA_DK, lane < (hh + 1) * GLA_DK)
                  for hh in range(GLA_HEADS)]
    blocks = [slice(c * GLA_CHUNK, (c + 1) * GLA_CHUNK) for c in range(n_ch)]

    def stack_heads(a):
        return _bf(jnp.concatenate([jnp.where(m, a, 0.0) for m in head_masks], axis=0))

    g = _log_decay(aqk_sc[win, 0:LANES], wgu_ref, bg_ref)
    ghi, gmid, glo = _split3(g)
    b = jnp.concatenate(
        [_dot(tril, ghi[r0:r0 + span]) + _dot(tril, gmid[r0:r0 + span]) + _dot(tril, glo[r0:r0 + span])
         for r0 in range(0, tile, span)], axis=0)
    last = [b[r.stop - 1:r.stop, :] for r in blocks]
    b_last = jnp.concatenate([jnp.broadcast_to(l, (GLA_CHUNK, GLA_KEY_WIDTH)) for l in last], axis=0)
    q = aqk_sc[win, q0:k0] * (GLA_DK ** -0.5)
    k = aqk_sc[win, k0:k0 + GLA_KEY_WIDTH]
    q_dec = q * jnp.exp(b)
    k_inv = _bf(k * jnp.exp(-b))
    k_rem = k * jnp.exp(b_last - b)

    q_stack = [stack_heads(q_dec[r]) for r in blocks]
    k_stack = [stack_heads(k_rem[r]) for r in blocks]
    scores = [jnp.where(causal, _dot_nt(q_stack[c], k_inv[blocks[c]]), 0.0) for c in range(n_ch)]
    tile_rows = [slice(base + r.start, base + r.stop) for r in blocks]
    v_blk = [v_sc[r, :] for r in tile_rows]
    v_heads = [[vb[:, hh * GLA_DV:(hh + 1) * GLA_DV] for hh in range(GLA_HEADS)] for vb in v_blk]
    kv = [_dot(_bf(jnp.concatenate(v_heads[c], axis=0).T), k_stack[c]) for c in range(n_ch)]
    o_intra = [jnp.concatenate(
        [_dot(_bf(scores[c][hh * GLA_CHUNK:(hh + 1) * GLA_CHUNK]), _bf(v_heads[c][hh]))
         for hh in range(GLA_HEADS)], axis=0) for c in range(n_ch)]

    st = st_sc[...]
    states = []
    for c in range(n_ch):
        states.append(_bf(st))
        st = jnp.exp(last[c]) * st + kv[c]
    st_sc[...] = st

    for c in range(n_ch):
        o = _dot_nt(q_stack[c], states[c]) + o_intra[c]
        o = o * lax.rsqrt(jnp.mean(o * o, axis=-1, keepdims=True) + EPS)
        for hh in range(GLA_HEADS):
            cols = slice(hh * GLA_DV, (hh + 1) * GLA_DV)
            oh = o[hh * GLA_CHUNK:(hh + 1) * GLA_CHUNK] * gnw_ref[:, cols]
            gate = g_sc[tile_rows[c], cols]
            oab_sc[tile_rows[c], cols] = _bf(oh * _silu(gate))


def _mixer_prompt_kernel(x_ref, n1_ref, win_ref, wgu_ref, bg_ref, gnw_ref, cw_ref, cb_ref,
                         lnw_ref, lnb_ref, wout_ref,
                         xo_ref, gla_ref, conv_ref,
                         aqk_sc, v_sc, g_sc, st_sc, xp_sc, oab_sc, h_sc, cv_sc, *, tile, n_tiles):
    i = pl.program_id(1)

    @pl.when(i == 0)
    def _():
        st_sc[...] = jnp.zeros_like(st_sc)
        xp_sc[:, 0:CONV_HIST, :] = jnp.zeros((CONV_BLOCKS, CONV_HIST, LANES), jnp.float32)

    h_sc[...] = _bf(_rms(x_ref[...], n1_ref[...]))
    rows = min(tile, MIXER_PASS)
    passes = list(range(0, tile, rows))

    def project(base, c0, c1):
        return _dot(h_sc[base:base + rows, :], win_ref[:, c0:c1])

    def project_glu(base):
        u = project(base, C_U, C_A)
        for cb in range(CONV_BLOCKS):
            c0 = cb * LANES
            xp_sc[cb, CONV_HIST + base:CONV_HIST + base + rows, :] = (
                u[:, c0:c0 + LANES] * jax.nn.sigmoid(u[:, CONV_CH + c0:CONV_CH + c0 + LANES]))

    def project_aqk(base):
        aqk_sc[base:base + rows, :] = project(base, C_A, C_V)

    def project_v(base):
        v_sc[base:base + rows, :] = project(base, C_V, C_G)

    def project_g(base):
        g_sc[base:base + rows, :] = project(base, C_G, IN_PAD)

    def conv(cb, base):
        _conv_block(cb, xp_sc, cv_sc, cw_ref, cb_ref, base, rows)

    def layer_norm_swish(base):
        win = slice(base, base + rows)
        cv = jnp.concatenate([cv_sc[cb, win, :] for cb in range(CONV_BLOCKS)], axis=1)
        mu = jnp.mean(cv, axis=-1, keepdims=True)
        var = jnp.mean(jnp.square(cv - mu), axis=-1, keepdims=True)
        y = (cv - mu) * lax.rsqrt(var + EPS) * lnw_ref[...] + lnb_ref[...]
        oab_sc[win, GLA_WIDTH:GLA_WIDTH + CONV_CH] = _bf(_silu(y))

    def finish(base):
        win = slice(base, base + rows)
        xo_ref[win, :] = x_ref[win, :] + _dot(oab_sc[win, :], wout_ref[...])

    for stage in (project_glu, project_aqk, project_v, project_g):
        stage(passes[0])
    for idx, base in enumerate(passes):
        ahead = passes[idx + 1] if idx + 1 < len(passes) else None
        if ahead is not None:
            project_glu(ahead)
        conv(0, base)
        conv(1, base)
        if ahead is not None:
            project_aqk(ahead)
        conv(2, base)
        conv(3, base)
        if ahead is not None:
            project_v(ahead)
        layer_norm_swish(base)
        _gla_rows(aqk_sc, v_sc, g_sc, wgu_ref, bg_ref, gnw_ref, st_sc, oab_sc, base, rows)
        if ahead is not None:
            project_g(ahead)
        finish(base)
    xp_sc[:, 0:CONV_HIST, :] = xp_sc[:, tile:tile + CONV_HIST, :]

    @pl.when(i == n_tiles - 1)
    def _():
        gla_ref[...] = st_sc[...].T.reshape(GLA_HEADS, GLA_DK, GLA_DV)
        for cb in range(CONV_BLOCKS):
            conv_ref[:, cb * LANES:(cb + 1) * LANES] = xp_sc[cb, CONV_FIRST:CONV_HIST, :]


def _mixer_prompt(x, layer, p):
    bsz, t, _ = x.shape
    tile = MIXER_TILE
    n_tiles = t // tile
    kern = functools.partial(_mixer_prompt_kernel, tile=tile, n_tiles=n_tiles)
    return pl.pallas_call(
        kern,
        grid=(bsz, n_tiles),
        in_specs=[
            pl.BlockSpec((None, tile, D_MODEL), lambda b, i: (b, i, 0)),
            _const_spec((1, D_MODEL), layer),
            _const_spec((D_MODEL, IN_PAD), layer),
            _const_spec((LANES, GLA_KEY_WIDTH), layer),
            _const_spec((1, GLA_KEY_WIDTH), layer),
            _const_spec((1, GLA_WIDTH), layer),
            _const_spec((CONV_WIDTH, CONV_CH), layer),
            _const_spec((1, CONV_CH), layer),
            _const_spec((1, CONV_CH), layer),
            _const_spec((1, CONV_CH), layer),
            _const_spec((D_MODEL, D_MODEL), layer),
        ],
        out_specs=[
            pl.BlockSpec((None, tile, D_MODEL), lambda b, i: (b, i, 0)),
            pl.BlockSpec((None, GLA_HEADS, GLA_DK, GLA_DV), lambda b, i: (b, 0, 0, 0)),
            pl.BlockSpec((None, CONV_WIDTH - 1, CONV_CH), lambda b, i: (b, 0, 0)),
        ],
        out_shape=[
            jax.ShapeDtypeStruct((bsz, t, D_MODEL), jnp.float32),
            jax.ShapeDtypeStruct((bsz, GLA_HEADS, GLA_DK, GLA_DV), jnp.float32),
            jax.ShapeDtypeStruct((bsz, CONV_WIDTH - 1, CONV_CH), jnp.float32),
        ],
        scratch_shapes=[
            pltpu.VMEM((tile, C_V - C_A), jnp.float32),
            pltpu.VMEM((tile, GLA_WIDTH), jnp.float32),
            pltpu.VMEM((tile, GLA_WIDTH), jnp.float32),
            pltpu.VMEM((GLA_DV, GLA_KEY_WIDTH), jnp.float32),
            pltpu.VMEM((CONV_BLOCKS, CONV_HIST + tile, LANES), jnp.float32),
            pltpu.VMEM((tile, D_MODEL), jnp.bfloat16),
            pltpu.VMEM((tile, D_MODEL), jnp.bfloat16),
            pltpu.VMEM((CONV_BLOCKS, tile, LANES), jnp.float32),
        ],
        compiler_params=pltpu.CompilerParams(
            dimension_semantics=("arbitrary", "arbitrary"), vmem_limit_bytes=VMEM_LIMIT),
        name="mixer_prompt",
    )(x, p["norm1_w"], p["w_in"], p["w_gate_up"], p["b_gate"], p["gla_norm_w"], p["conv_w"],
      p["conv_b"], p["conv_ln_w"], p["conv_ln_b"], p["w_out"])


FF_BLOCKS = D_FF // LANES


FF_GROUP = 2


def _ffn_gate(up_r, cb, fw_ref, fb_ref, tile):
    def conv(blk):
        cols = slice(blk * LANES, (blk + 1) * LANES)
        acc = jnp.broadcast_to(fb_ref[:, cols], (tile, LANES))
        for j in range(FFN_CONV_WIDTH):
            r = FFN_HIST - (FFN_CONV_WIDTH - 1) + j
            acc = acc + fw_ref[j:j + 1, cols] * up_r[blk, r:r + tile, :]
        return acc

    return _bf(_silu(conv(cb)) * conv(FF_BLOCKS + cb))


def _ffn_prompt_kernel(x_ref, n2_ref, wup_ref, fw_ref, fb_ref, wdown_ref, nf_ref,
                       xo_ref, ffn_ref, up_sc, h_sc, gated_sc, *, tile, n_tiles, final_norm):
    i = pl.program_id(1)

    @pl.when(i == 0)
    def _():
        up_sc[:, 0:FFN_HIST, :] = jnp.zeros((2 * FF_BLOCKS, FFN_HIST, LANES), jnp.float32)

    x = x_ref[...]
    h_sc[...] = _bf(_rms(x, n2_ref[...]))
    xo_ref[...] = x
    width = FF_GROUP * LANES
    n_groups = FF_BLOCKS // FF_GROUP

    def up_group(grp):
        for half in range(2):
            c0 = half * D_FF + grp * width
            up = _dot(h_sc[...], wup_ref[:, c0:c0 + width])
            for j in range(FF_GROUP):
                up_sc[half * FF_BLOCKS + grp * FF_GROUP + j, FFN_HIST:FFN_HIST + tile, :] = (
                    up[:, j * LANES:(j + 1) * LANES])

    def gate_group(grp):
        for j in range(FF_GROUP):
            cb = grp * FF_GROUP + j
            gated_sc[:, cb * LANES:(cb + 1) * LANES] = _ffn_gate(up_sc, cb, fw_ref, fb_ref, tile)

    def down_group(grp):
        rows = slice(grp * width, (grp + 1) * width)
        xo_ref[...] += _dot(gated_sc[:, rows], wdown_ref[rows, :])

    for step in range(n_groups + 2):
        if step < n_groups:
            up_group(step)
        if 1 <= step <= n_groups:
            gate_group(step - 1)
        if step >= 2:
            down_group(step - 2)
    if final_norm:
        xo_ref[...] = _rms(xo_ref[...], nf_ref[...])

    @pl.when(i == n_tiles - 1)
    def _():
        for cb in range(2 * FF_BLOCKS):
            ffn_ref[:, cb * LANES:(cb + 1) * LANES] = up_sc[
                cb, FFN_HIST + tile - (FFN_CONV_WIDTH - 1):FFN_HIST + tile, :]

    up_sc[:, 0:FFN_HIST, :] = up_sc[:, tile:tile + FFN_HIST, :]


def _ffn_prompt(x, layer, p, final_norm):
    bsz, t, _ = x.shape
    tile = FFN_TILE
    n_tiles = t // tile
    kern = functools.partial(_ffn_prompt_kernel, tile=tile, n_tiles=n_tiles, final_norm=final_norm)
    return pl.pallas_call(
        kern,
        grid=(bsz, n_tiles),
        in_specs=[
            pl.BlockSpec((None, tile, D_MODEL), lambda b, i: (b, i, 0)),
            _const_spec((1, D_MODEL), layer),
            _const_spec((D_MODEL, 2 * D_FF), layer),
            _const_spec((FFN_CONV_WIDTH, 2 * D_FF), layer),
            _const_spec((1, 2 * D_FF), layer),
            _const_spec((D_FF, D_MODEL), layer),
            pl.BlockSpec((1, D_MODEL), lambda b, i: (0, 0), pipeline_mode=pl.Buffered(1)),
        ],
        out_specs=[
            pl.BlockSpec((None, tile, D_MODEL), lambda b, i: (b, i, 0)),
            pl.BlockSpec((None, FFN_CONV_WIDTH - 1, 2 * D_FF), lambda b, i: (b, 0, 0)),
        ],
        out_shape=[
            jax.ShapeDtypeStruct((bsz, t, D_MODEL), jnp.float32),
            jax.ShapeDtypeStruct((bsz, FFN_CONV_WIDTH - 1, 2 * D_FF), jnp.float32),
        ],
        scratch_shapes=[
            pltpu.VMEM((2 * FF_BLOCKS, FFN_HIST + tile, LANES), jnp.float32),
            pltpu.VMEM((tile, D_MODEL), jnp.bfloat16),
            pltpu.VMEM((tile, D_FF), jnp.bfloat16),
        ],
        compiler_params=pltpu.CompilerParams(
            dimension_semantics=("arbitrary", "arbitrary"), vmem_limit_bytes=VMEM_LIMIT),
        name="ffn_prompt",
    )(x, p["norm2_w"], p["w_up"], p["ffn_conv_w"], p["ffn_conv_b"], p["w_down"], p["norm_f_w"])


def _proj_sample_kernel(x_ref, n1_ref, win_ref, wgu_ref, bg_ref, cs_ref, cw_ref, cb_ref,
                        lnw_ref, lnb_ref,
                        qd_ref, k_ref, a_ref, v_ref, sc_ref, g_ref, glu_ref, ob_ref):
    x = x_ref[...]
    proj = _dot(_bf(_rms(x, n1_ref[...])), win_ref[...])
    g = _log_decay(proj[:, C_A:C_A + LANES], wgu_ref, bg_ref)
    q = proj[:, C_Q:C_K] * (GLA_DK ** -0.5)
    k = proj[:, C_K:C_V]
    q_dec = q * jnp.exp(g)
    k_inv = k * jnp.exp(-g)
    qd_ref[...] = q_dec
    k_ref[...] = k * jnp.exp(g - g)
    a_ref[...] = jnp.exp(g)
    v_ref[...] = proj[:, C_V:C_G]
    g_ref[...] = proj[:, C_G:IN_PAD]
    qk = q_dec * k_inv
    for hh in range(GLA_HEADS):
        sc_ref[:, hh * LANES:(hh + 1) * LANES] = jnp.broadcast_to(
            jnp.sum(qk[:, hh * GLA_DK:(hh + 1) * GLA_DK], axis=-1, keepdims=True),
            (x.shape[0], LANES))
    glu = proj[:, C_U:C_U + CONV_CH] * jax.nn.sigmoid(proj[:, C_U + CONV_CH:C_A])
    glu_ref[...] = glu
    acc = cb_ref[...] + cw_ref[CONV_WIDTH - 1:CONV_WIDTH, :] * glu
    for j in range(CONV_WIDTH - 1):
        acc = acc + cw_ref[j:j + 1, :] * cs_ref[j]
    mu = jnp.mean(acc, axis=-1, keepdims=True)
    var = jnp.mean(jnp.square(acc - mu), axis=-1, keepdims=True)
    y = (acc - mu) * lax.rsqrt(var + EPS) * lnw_ref[...] + lnb_ref[...]
    ob_ref[...] = _silu(y)


def _proj_sample(x, cs_t, layer, p):
    n = x.shape[0]
    full = lambda shape: pl.BlockSpec(shape, lambda i: (0,) * len(shape))
    f32 = jnp.float32
    return pl.pallas_call(
        _proj_sample_kernel,
        grid=(1,),
        in_specs=[
            full((n, D_MODEL)),
            _const_spec((1, D_MODEL), layer),
            _const_spec((D_MODEL, IN_PAD), layer),
            _const_spec((LANES, GLA_KEY_WIDTH), layer),
            _const_spec((1, GLA_KEY_WIDTH), layer),
            pl.BlockSpec((None, CONV_WIDTH - 1, n, CONV_CH), lambda i: (layer, 0, 0, 0)),
            _const_spec((CONV_WIDTH, CONV_CH), layer),
            _const_spec((1, CONV_CH), layer),
            _const_spec((1, CONV_CH), layer),
            _const_spec((1, CONV_CH), layer),
        ],
        out_specs=[full((n, GLA_KEY_WIDTH)), full((n, GLA_KEY_WIDTH)), full((n, GLA_KEY_WIDTH)),
                   full((n, GLA_WIDTH)), full((n, GLA_WIDTH)), full((n, GLA_WIDTH)),
                   full((n, CONV_CH)), full((n, CONV_CH))],
        out_shape=[jax.ShapeDtypeStruct((n, GLA_KEY_WIDTH), f32)] * 3
        + [jax.ShapeDtypeStruct((n, GLA_WIDTH), f32)] * 3
        + [jax.ShapeDtypeStruct((n, CONV_CH), f32)] * 2,
        compiler_params=pltpu.CompilerParams(
            dimension_semantics=("arbitrary",), vmem_limit_bytes=VMEM_LIMIT),
        name="proj_sample",
    )(x, p["norm1_w"], p["w_in"], p["w_gate_up"], p["b_gate"], cs_t, p["conv_w"], p["conv_b"],
      p["conv_ln_w"], p["conv_ln_b"])


def _gla_sample_kernel(qd_ref, k_ref, a_ref, v_ref, s_ref, o_ref, so_ref):
    def column(ref, bb):
        return jnp.broadcast_to(ref[bb], (LANES, GLA_KEY_WIDTH)).T

    def body(bb, carry):
        qc = column(qd_ref, bb)
        kc = column(k_ref, bb)
        ac = column(a_ref, bb)
        s = s_ref[bb].reshape(GLA_HEADS * GLA_DK, GLA_DV)
        v = v_ref[bb]
        vb = jnp.concatenate(
            [jnp.broadcast_to(v[:, hh * GLA_DV:(hh + 1) * GLA_DV], (GLA_DK, GLA_DV))
             for hh in range(GLA_HEADS)], axis=0)
        qs = qc * s
        o_ref[bb] = jnp.concatenate(
            [jnp.sum(qs[hh * GLA_DK:(hh + 1) * GLA_DK], axis=0, keepdims=True)
             for hh in range(GLA_HEADS)], axis=1)
        so_ref[bb] = (ac * s + kc * vb).reshape(GLA_HEADS, GLA_DK, GLA_DV)
        return carry

    lax.fori_loop(0, SAMPLE_GLA_BLOCK, body, 0, unroll=SAMPLE_GLA_UNROLL)


def _gla_sample(qd, k, a, v, s_all, layer):
    n = qd.shape[0]
    blk = SAMPLE_GLA_BLOCK
    vec = lambda w: pl.BlockSpec((blk, 1, w), lambda i: (i, 0, 0))
    sspec = pl.BlockSpec((None, blk, GLA_HEADS, GLA_DK, GLA_DV), lambda i: (layer, i, 0, 0, 0))
    o, s_new = pl.pallas_call(
        _gla_sample_kernel,
        grid=(n // blk,),
        in_specs=[vec(GLA_KEY_WIDTH), vec(GLA_KEY_WIDTH), vec(GLA_KEY_WIDTH), vec(GLA_WIDTH), sspec],
        out_specs=[vec(GLA_WIDTH), sspec],
        out_shape=[jax.ShapeDtypeStruct((n, 1, GLA_WIDTH), jnp.float32),
                   jax.ShapeDtypeStruct(s_all.shape, jnp.float32)],
        input_output_aliases={4: 1},
        compiler_params=pltpu.CompilerParams(
            dimension_semantics=("arbitrary",), vmem_limit_bytes=VMEM_LIMIT),
        name="gla_sample",
    )(qd[:, None, :], k[:, None, :], a[:, None, :], v[:, None, :], s_all)
    return o[:, 0, :], s_new


def _tail_sample_kernel(x_ref, oi_ref, sc_ref, v_ref, g_ref, ob_ref, gnw_ref, wout_ref,
                        n2_ref, wup_ref, st0_ref, st1_ref, fw_ref, fb_ref, wdown_ref, nf_ref,
                        xo_ref, up_ref, *, final_norm):
    parts = []
    for hh in range(GLA_HEADS):
        cols = slice(hh * GLA_DV, (hh + 1) * GLA_DV)
        o = oi_ref[:, cols] + sc_ref[:, cols] * v_ref[:, cols]
        o = o * lax.rsqrt(jnp.mean(o * o, axis=-1, keepdims=True) + EPS)
        parts.append(_bf(o * gnw_ref[:, cols] * _silu(g_ref[:, cols])))
    parts.append(_bf(ob_ref[...]))
    x = x_ref[...] + _dot(jnp.concatenate(parts, axis=1), wout_ref[...])

    h2 = _bf(_rms(x, n2_ref[...]))
    up = _dot(h2, wup_ref[...])
    up_ref[...] = up
    upc = (fb_ref[...] + fw_ref[0:1, :] * st0_ref[...] + fw_ref[1:2, :] * st1_ref[...]
           + fw_ref[2:3, :] * up)
    gated = _bf(_silu(upc[:, :D_FF]) * upc[:, D_FF:])
    y = x + _dot(gated, wdown_ref[...])
    if final_norm:
        y = _rms(y, nf_ref[...])
    xo_ref[...] = y


def _tail_sample(x, oi, sc, v, g, ob, ffn_t, layer, p, final_norm):
    n = x.shape[0]
    full = lambda shape: pl.BlockSpec(shape, lambda i: (0,) * len(shape))
    plane = lambda j: pl.BlockSpec((None, None, n, 2 * D_FF), lambda i: (layer, j, 0, 0))
    return pl.pallas_call(
        functools.partial(_tail_sample_kernel, final_norm=final_norm),
        grid=(1,),
        in_specs=[
            full((n, D_MODEL)), full((n, GLA_WIDTH)), full((n, GLA_WIDTH)), full((n, GLA_WIDTH)),
            full((n, GLA_WIDTH)), full((n, CONV_CH)),
            _const_spec((1, GLA_WIDTH), layer),
            _const_spec((D_MODEL, D_MODEL), layer),
            _const_spec((1, D_MODEL), layer),
            _const_spec((D_MODEL, 2 * D_FF), layer),
            plane(0), plane(1),
            _const_spec((FFN_CONV_WIDTH, 2 * D_FF), layer),
            _const_spec((1, 2 * D_FF), layer),
            _const_spec((D_FF, D_MODEL), layer),
            pl.BlockSpec((1, D_MODEL), lambda i: (0, 0)),
        ],
        out_specs=[full((n, D_MODEL)), full((n, 2 * D_FF))],
        out_shape=[jax.ShapeDtypeStruct((n, D_MODEL), jnp.float32),
                   jax.ShapeDtypeStruct((n, 2 * D_FF), jnp.float32)],
        compiler_params=pltpu.CompilerParams(
            dimension_semantics=("arbitrary",), vmem_limit_bytes=VMEM_LIMIT),
        name="tail_sample",
    )(x, oi, sc, v, g, ob, p["gla_norm_w"], p["w_out"], p["norm2_w"], p["w_up"], ffn_t, ffn_t,
      p["ffn_conv_w"], p["ffn_conv_b"], p["w_down"], p["norm_f_w"])


W_IN_ROWS = 256


def _w_in_layout_kernel(w_ref, o_ref):
    a0 = 2 * GLA_KEY_WIDTH + 2 * GLA_WIDTH
    tail = w_ref[:, a0:]
    o_ref[:, C_U:C_A] = _bf(tail[:, GATE_RANK:])
    lane = lax.broadcasted_iota(jnp.int32, (W_IN_ROWS, LANES), 1)
    o_ref[:, C_A:C_Q] = _bf(jnp.where(lane < GATE_RANK, tail[:, 0:LANES], 0.0))
    o_ref[:, C_Q:IN_PAD] = _bf(w_ref[:, 0:a0])


def _w_in_layout(w_in):
    depth, d, cols = w_in.shape
    return pl.pallas_call(
        _w_in_layout_kernel,
        grid=(depth, d // W_IN_ROWS),
        in_specs=[pl.BlockSpec((None, W_IN_ROWS, cols), lambda l, r: (l, r, 0))],
        out_specs=pl.BlockSpec((None, W_IN_ROWS, IN_PAD), lambda l, r: (l, r, 0)),
        out_shape=jax.ShapeDtypeStruct((depth, d, IN_PAD), jnp.bfloat16),
        compiler_params=pltpu.CompilerParams(dimension_semantics=("arbitrary", "arbitrary")),
        name="w_in_layout",
    )(w_in)


def _prepare_params(norm1_w, w_in, w_gate_up, b_gate, gla_norm_w, conv_w, conv_b, conv_ln_w,
                    conv_ln_b, w_out, norm2_w, w_up, ffn_conv_w, ffn_conv_b, w_down, norm_f_w):
    depth = w_in.shape[0]
    wgu = jnp.concatenate(
        [w_gate_up, jnp.zeros((depth, LANES - GATE_RANK, GLA_KEY_WIDTH), w_gate_up.dtype)], axis=1)
    row = lambda a: a[:, None, :]
    return {
        "norm1_w": row(norm1_w), "w_in": _w_in_layout(w_in), "w_gate_up": _bf(wgu),
        "b_gate": row(b_gate),
        "gla_norm_w": row(gla_norm_w), "conv_w": conv_w, "conv_b": row(conv_b),
        "conv_ln_w": row(conv_ln_w), "conv_ln_b": row(conv_ln_b), "w_out": _bf(w_out),
        "norm2_w": row(norm2_w), "w_up": _bf(w_up), "ffn_conv_w": ffn_conv_w,
        "ffn_conv_b": row(ffn_conv_b), "w_down": _bf(w_down), "norm_f_w": norm_f_w[None, :],
    }


def kernel(x_prompt, x_sample, state_gla, state_conv, state_ffn, norm1_w, w_in, w_gate_up, b_gate,
           gla_norm_w, conv_w, conv_b, conv_ln_w, conv_ln_b, w_out, norm2_w, w_up, ffn_conv_w,
           ffn_conv_b, w_down, norm_f_w):
    p = _prepare_params(norm1_w, w_in, w_gate_up, b_gate, gla_norm_w, conv_w, conv_b, conv_ln_w,
                        conv_ln_b, w_out, norm2_w, w_up, ffn_conv_w, ffn_conv_b, w_down, norm_f_w)
    depth = w_in.shape[0]

    x = x_prompt
    gla_p, conv_p, ffn_p = [], [], []
    for l in range(depth):
        x, s_new, c_buf = _mixer_prompt(x, l, p)
        x, f_buf = _ffn_prompt(x, l, p, final_norm=(l == depth - 1))
        gla_p.append(s_new)
        conv_p.append(c_buf)
        ffn_p.append(f_buf)
    y_prompt = x

    xs = x_sample[:, 0, :]
    conv_t = jnp.swapaxes(state_conv, 1, 2)
    ffn_t = jnp.swapaxes(state_ffn, 1, 2)
    gla_s = state_gla
    glus, ups = [], []
    for l in range(depth):
        qd, k, a, v, sc, g, glu, ob = _proj_sample(xs, conv_t, l, p)
        oi, gla_s = _gla_sample(qd, k, a, v, gla_s, l)
        xs, up = _tail_sample(xs, oi, sc, v, g, ob, ffn_t, l, p, final_norm=(l == depth - 1))
        glus.append(glu)
        ups.append(up)
    y_sample = xs[:, None, :]
    conv_s = jnp.concatenate([state_conv[:, :, 1:, :], jnp.stack(glus)[:, :, None, :]], axis=2)
    ffn_s = jnp.concatenate([state_ffn[:, :, 1:, :], jnp.stack(ups)[:, :, None, :]], axis=2)

    return (y_prompt, y_sample, jnp.stack(gla_p), gla_s, jnp.stack(conv_p), conv_s,
            jnp.stack(ffn_p), ffn_s)
```

```python
import functools

import jax
import jax.numpy as jnp
from jax import lax
from jax.experimental import pallas as pl
from jax.experimental.pallas import tpu as pltpu

D_MODEL = 1024
GLA_WIDTH = 512
CONV_CH = 512
GLA_HEADS = 4
GLA_DV = 128
GLA_KEY_WIDTH = 256
GLA_DK = 64
GATE_RANK = 16
GATE_TAU = 16.0
GLA_CHUNK = 64
CONV_WIDTH = 31
FFN_CONV_WIDTH = 3
D_FF = 2816
EPS = 1e-6

LANES = 128
SUBLANES = 8

CONV_BLOCKS = CONV_CH // LANES
C_U = 0
C_A = C_U + 2 * CONV_CH
C_Q = C_A + LANES
C_K = C_Q + GLA_KEY_WIDTH
C_V = C_K + GLA_KEY_WIDTH
C_G = C_V + GLA_WIDTH
IN_PAD = C_G + GLA_WIDTH

CONV_HIST = 32
FFN_HIST = SUBLANES
MIXER_TILE = 512
MIXER_PASS = 512
FFN_TILE = 256
CUMSUM_ROWS = 256
CONV_ROWS = 64
SAMPLE_GLA_BLOCK = 16
SAMPLE_GLA_UNROLL = 4

VMEM_LIMIT = 56 * 1024 * 1024


def _rms(x, w):
    return x * lax.rsqrt(jnp.mean(x * x, axis=-1, keepdims=True) + EPS) * w


def _log_sigmoid(z):
    return jnp.minimum(z, 0.0) - jnp.log1p(jnp.exp(-jnp.abs(z)))


def _silu(x):
    return x * jax.nn.sigmoid(x)


def _bf(x):
    return x.astype(jnp.bfloat16)


def _dot(a, b):
    return jnp.dot(a, b, preferred_element_type=jnp.float32)


def _dot_nt(a, b):
    return lax.dot_general(a, b, (((1,), (1,)), ((), ())), preferred_element_type=jnp.float32)


def _split3(x):
    hi = _bf(x)
    r = x - hi.astype(jnp.float32)
    mid = _bf(r)
    lo = _bf(r - mid.astype(jnp.float32))
    return hi, mid, lo


def _log_decay(alow, wgu_ref, bg_ref):
    z = _dot(_bf(alow), wgu_ref[...]) + bg_ref[...]
    return _log_sigmoid(z) * (1.0 / GATE_TAU)


def _const_spec(shape, layer):
    nd = len(shape)
    return pl.BlockSpec((None,) + tuple(shape), lambda *_: (layer,) + (0,) * nd,
                        pipeline_mode=pl.Buffered(1))


CONV_FIRST = CONV_HIST - (CONV_WIDTH - 1)


def _conv_block(cb, xp_sc, cv_sc, cw_ref, cb_ref, base, rows):
    cols = slice(cb * LANES, (cb + 1) * LANES)
    for r0 in range(base, base + rows, CONV_ROWS):
        acc = jnp.broadcast_to(cb_ref[:, cols], (CONV_ROWS, LANES))
        for j in range(CONV_WIDTH):
            s = r0 + CONV_FIRST + j
            acc = acc + cw_ref[j:j + 1, cols] * xp_sc[cb, s:s + CONV_ROWS, :]
        cv_sc[cb, r0:r0 + CONV_ROWS, :] = acc


def _gla_rows(aqk_sc, v_sc, g_sc, wgu_ref, bg_ref, gnw_ref, st_sc, oab_sc, base, rows):
    q0 = C_Q - C_A
    k0 = C_K - C_A
    tile = rows
    n_ch = tile // GLA_CHUNK
    span = min(tile, CUMSUM_ROWS)
    win = slice(base, base + rows)
    trow = lax.broadcasted_iota(jnp.int32, (span, span), 0)
    tcol = lax.broadcasted_iota(jnp.int32, (span, span), 1)
    same_block = jnp.bitwise_and(trow, -GLA_CHUNK) == jnp.bitwise_and(tcol, -GLA_CHUNK)
    tril = _bf(jnp.where(jnp.logical_and(same_block, tcol <= trow), 1.0, 0.0))
    srow = lax.broadcasted_iota(jnp.int32, (GLA_HEADS * GLA_CHUNK, GLA_CHUNK), 0)
    scol = lax.broadcasted_iota(jnp.int32, (GLA_HEADS * GLA_CHUNK, GLA_CHUNK), 1)
    causal = scol <= jnp.bitwise_and(srow, GLA_CHUNK - 1)
    lane = lax.broadcasted_iota(jnp.int32, (1, GLA_KEY_WIDTH), 1)
    head_masks = [jnp.logical_and(lane >= hh * GLA_DK, lane < (hh + 1) * GLA_DK)
                  for hh in range(GLA_HEADS)]
    blocks = [slice(c * GLA_CHUNK, (c + 1) * GLA_CHUNK) for c in range(n_ch)]

    def stack_heads(a):
        return _bf(jnp.concatenate([jnp.where(m, a, 0.0) for m in head_masks], axis=0))

    g = _log_decay(aqk_sc[win, 0:LANES], wgu_ref, bg_ref)
    ghi, gmid, glo = _split3(g)
    b = jnp.concatenate(
        [_dot(tril, ghi[r0:r0 + span]) + _dot(tril, gmid[r0:r0 + span]) + _dot(tril, glo[r0:r0 + span])
         for r0 in range(0, tile, span)], axis=0)
    last = [b[r.stop - 1:r.stop, :] for r in blocks]
    b_last = jnp.concatenate([jnp.broadcast_to(l, (GLA_CHUNK, GLA_KEY_WIDTH)) for l in last], axis=0)
    q = aqk_sc[win, q0:k0] * (GLA_DK ** -0.5)
    k = aqk_sc[win, k0:k0 + GLA_KEY_WIDTH]
    q_dec = q * jnp.exp(b)
    k_inv = _bf(k * jnp.exp(-b))
    k_rem = k * jnp.exp(b_last - b)

    q_stack = [stack_heads(q_dec[r]) for r in blocks]
    k_stack = [stack_heads(k_rem[r]) for r in blocks]
    scores = [jnp.where(causal, _dot_nt(q_stack[c], k_inv[blocks[c]]), 0.0) for c in range(n_ch)]
    tile_rows = [slice(base + r.start, base + r.stop) for r in blocks]
    v_blk = [v_sc[r, :] for r in tile_rows]
    v_heads = [[vb[:, hh * GLA_DV:(hh + 1) * GLA_DV] for hh in range(GLA_HEADS)] for vb in v_blk]
    kv = [_dot(_bf(jnp.concatenate(v_heads[c], axis=0).T), k_stack[c]) for c in range(n_ch)]
    o_intra = [jnp.concatenate(
        [_dot(_bf(scores[c][hh * GLA_CHUNK:(hh + 1) * GLA_CHUNK]), _bf(v_heads[c][hh]))
         for hh in range(GLA_HEADS)], axis=0) for c in range(n_ch)]

    st = st_sc[...]
    states = []
    for c in range(n_ch):
        states.append(_bf(st))
        st = jnp.exp(last[c]) * st + kv[c]
    st_sc[...] = st

    for c in range(n_ch):
        o = _dot_nt(q_stack[c], states[c]) + o_intra[c]
        o = o * lax.rsqrt(jnp.mean(o * o, axis=-1, keepdims=True) + EPS)
        for hh in range(GLA_HEADS):
            cols = slice(hh * GLA_DV, (hh + 1) * GLA_DV)
            oh = o[hh * GLA_CHUNK:(hh + 1) * GLA_CHUNK] * gnw_ref[:, cols]
            gate = g_sc[tile_rows[c], cols]
            oab_sc[tile_rows[c], cols] = _bf(oh * _silu(gate))


def _mixer_prompt_kernel(x_ref, n1_ref, win_ref, wgu_ref, bg_ref, gnw_ref, cw_ref, cb_ref,
                         lnw_ref, lnb_ref, wout_ref,
                         xo_ref, gla_ref, conv_ref,
                         aqk_sc, v_sc, g_sc, st_sc, xp_sc, oab_sc, h_sc, cv_sc, *, tile, n_tiles):
    i = pl.program_id(1)

    @pl.when(i == 0)
    def _():
        st_sc[...] = jnp.zeros_like(st_sc)
        xp_sc[:, 0:CONV_HIST, :] = jnp.zeros((CONV_BLOCKS, CONV_HIST, LANES), jnp.float32)

    h_sc[...] = _bf(_rms(x_ref[...], n1_ref[...]))
    rows = min(tile, MIXER_PASS)

    def project(c0, c1):
        return _dot(h_sc[...], win_ref[:, c0:c1])

    u = project(C_U, C_A)
    for cb in range(CONV_BLOCKS):
        c0 = cb * LANES
        xp_sc[cb, CONV_HIST:CONV_HIST + tile, :] = (
            u[:, c0:c0 + LANES] * jax.nn.sigmoid(u[:, CONV_CH + c0:CONV_CH + c0 + LANES]))
    aqk_sc[...] = project(C_A, C_V)
    v_sc[...] = project(C_V, C_G)
    g_sc[...] = project(C_G, IN_PAD)

    def conv(cb, base):
        _conv_block(cb, xp_sc, cv_sc, cw_ref, cb_ref, base, rows)

    def layer_norm_swish(base):
        win = slice(base, base + rows)
        cv = jnp.concatenate([cv_sc[cb, win, :] for cb in range(CONV_BLOCKS)], axis=1)
        mu = jnp.mean(cv, axis=-1, keepdims=True)
        var = jnp.mean(jnp.square(cv - mu), axis=-1, keepdims=True)
        y = (cv - mu) * lax.rsqrt(var + EPS) * lnw_ref[...] + lnb_ref[...]
        oab_sc[win, GLA_WIDTH:GLA_WIDTH + CONV_CH] = _bf(_silu(y))

    def finish(base):
        win = slice(base, base + rows)
        xo_ref[win, :] = x_ref[win, :] + _dot(oab_sc[win, :], wout_ref[...])

    for base in range(0, tile, rows):
        for cb in range(CONV_BLOCKS):
            conv(cb, base)
        layer_norm_swish(base)
        _gla_rows(aqk_sc, v_sc, g_sc, wgu_ref, bg_ref, gnw_ref, st_sc, oab_sc, base, rows)
        finish(base)
    xp_sc[:, 0:CONV_HIST, :] = xp_sc[:, tile:tile + CONV_HIST, :]

    @pl.when(i == n_tiles - 1)
    def _():
        gla_ref[...] = st_sc[...].T.reshape(GLA_HEADS, GLA_DK, GLA_DV)
        for cb in range(CONV_BLOCKS):
            conv_ref[:, cb * LANES:(cb + 1) * LANES] = xp_sc[cb, CONV_FIRST:CONV_HIST, :]


def _mixer_prompt(x, layer, p):
    bsz, t, _ = x.shape
    tile = MIXER_TILE
    n_tiles = t // tile
    kern = functools.partial(_mixer_prompt_kernel, tile=tile, n_tiles=n_tiles)
    return pl.pallas_call(
        kern,
        grid=(bsz, n_tiles),
        in_specs=[
            pl.BlockSpec((None, tile, D_MODEL), lambda b, i: (b, i, 0)),
            _const_spec((1, D_MODEL), layer),
            _const_spec((D_MODEL, IN_PAD), layer),
            _const_spec((LANES, GLA_KEY_WIDTH), layer),
            _const_spec((1, GLA_KEY_WIDTH), layer),
            _const_spec((1, GLA_WIDTH), layer),
            _const_spec((CONV_WIDTH, CONV_CH), layer),
            _const_spec((1, CONV_CH), layer),
            _const_spec((1, CONV_CH), layer),
            _const_spec((1, CONV_CH), layer),
            _const_spec((D_MODEL, D_MODEL), layer),
        ],
        out_specs=[
            pl.BlockSpec((None, tile, D_MODEL), lambda b, i: (b, i, 0)),
            pl.BlockSpec((None, GLA_HEADS, GLA_DK, GLA_DV), lambda b, i: (b, 0, 0, 0)),
            pl.BlockSpec((None, CONV_WIDTH - 1, CONV_CH), lambda b, i: (b, 0, 0)),
        ],
        out_shape=[
            jax.ShapeDtypeStruct((bsz, t, D_MODEL), jnp.float32),
            jax.ShapeDtypeStruct((bsz, GLA_HEADS, GLA_DK, GLA_DV), jnp.float32),
            jax.ShapeDtypeStruct((bsz, CONV_WIDTH - 1, CONV_CH), jnp.float32),
        ],
        scratch_shapes=[
            pltpu.VMEM((tile, C_V - C_A), jnp.float32),
            pltpu.VMEM((tile, GLA_WIDTH), jnp.float32),
            pltpu.VMEM((tile, GLA_WIDTH), jnp.float32),
            pltpu.VMEM((GLA_DV, GLA_KEY_WIDTH), jnp.float32),
            pltpu.VMEM((CONV_BLOCKS, CONV_HIST + tile, LANES), jnp.float32),
            pltpu.VMEM((tile, D_MODEL), jnp.bfloat16),
            pltpu.VMEM((tile, D_MODEL), jnp.bfloat16),
            pltpu.VMEM((CONV_BLOCKS, tile, LANES), jnp.float32),
        ],
        compiler_params=pltpu.CompilerParams(
            dimension_semantics=("arbitrary", "arbitrary"), vmem_limit_bytes=VMEM_LIMIT),
        name="mixer_prompt",
    )(x, p["norm1_w"], p["w_in"], p["w_gate_up"], p["b_gate"], p["gla_norm_w"], p["conv_w"],
      p["conv_b"], p["conv_ln_w"], p["conv_ln_b"], p["w_out"])


FF_BLOCKS = D_FF // LANES


FF_GROUP = 2


def _ffn_gate(up_r, cb, fw_ref, fb_ref, tile):
    def conv(blk):
        cols = slice(blk * LANES, (blk + 1) * LANES)
        acc = jnp.broadcast_to(fb_ref[:, cols], (tile, LANES))
        for j in range(FFN_CONV_WIDTH):
            r = FFN_HIST - (FFN_CONV_WIDTH - 1) + j
            acc = acc + fw_ref[j:j + 1, cols] * up_r[blk, r:r + tile, :]
        return acc

    return _bf(_silu(conv(cb)) * conv(FF_BLOCKS + cb))


def _ffn_prompt_kernel(x_ref, n2_ref, wup_ref, fw_ref, fb_ref, wdown_ref, nf_ref,
                       xo_ref, ffn_ref, up_sc, h_sc, gated_sc, *, tile, n_tiles, final_norm):
    i = pl.program_id(1)

    @pl.when(i == 0)
    def _():
        up_sc[:, 0:FFN_HIST, :] = jnp.zeros((2 * FF_BLOCKS, FFN_HIST, LANES), jnp.float32)

    x = x_ref[...]
    h_sc[...] = _bf(_rms(x, n2_ref[...]))
    xo_ref[...] = x
    width = FF_GROUP * LANES
    n_groups = FF_BLOCKS // FF_GROUP

    def up_group(grp):
        for half in range(2):
            c0 = half * D_FF + grp * width
            up = _dot(h_sc[...], wup_ref[:, c0:c0 + width])
            for j in range(FF_GROUP):
                up_sc[half * FF_BLOCKS + grp * FF_GROUP + j, FFN_HIST:FFN_HIST + tile, :] = (
                    up[:, j * LANES:(j + 1) * LANES])

    def gate_group(grp):
        for j in range(FF_GROUP):
            cb = grp * FF_GROUP + j
            gated_sc[:, cb * LANES:(cb + 1) * LANES] = _ffn_gate(up_sc, cb, fw_ref, fb_ref, tile)

    def down_group(grp):
        rows = slice(grp * width, (grp + 1) * width)
        xo_ref[...] += _dot(gated_sc[:, rows], wdown_ref[rows, :])

    for step in range(n_groups + 2):
        if step < n_groups:
            up_group(step)
        if 1 <= step <= n_groups:
            gate_group(step - 1)
        if step >= 2:
            down_group(step - 2)
    if final_norm:
        xo_ref[...] = _rms(xo_ref[...], nf_ref[...])

    @pl.when(i == n_tiles - 1)
    def _():
        for cb in range(2 * FF_BLOCKS):
            ffn_ref[:, cb * LANES:(cb + 1) * LANES] = up_sc[
                cb, FFN_HIST + tile - (FFN_CONV_WIDTH - 1):FFN_HIST + tile, :]

    up_sc[:, 0:FFN_HIST, :] = up_sc[:, tile:tile + FFN_HIST, :]


def _ffn_prompt(x, layer, p, final_norm):
    bsz, t, _ = x.shape
    tile = FFN_TILE
    n_tiles = t // tile
    kern = functools.partial(_ffn_prompt_kernel, tile=tile, n_tiles=n_tiles, final_norm=final_norm)
    return pl.pallas_call(
        kern,
        grid=(bsz, n_tiles),
        in_specs=[
            pl.BlockSpec((None, tile, D_MODEL), lambda b, i: (b, i, 0)),
            _const_spec((1, D_MODEL), layer),
            _const_spec((D_MODEL, 2 * D_FF), layer),
            _const_spec((FFN_CONV_WIDTH, 2 * D_FF), layer),
            _const_spec((1, 2 * D_FF), layer),
            _const_spec((D_FF, D_MODEL), layer),
            pl.BlockSpec((1, D_MODEL), lambda b, i: (0, 0), pipeline_mode=pl.Buffered(1)),
        ],
        out_specs=[
            pl.BlockSpec((None, tile, D_MODEL), lambda b, i: (b, i, 0)),
            pl.BlockSpec((None, FFN_CONV_WIDTH - 1, 2 * D_FF), lambda b, i: (b, 0, 0)),
        ],
        out_shape=[
            jax.ShapeDtypeStruct((bsz, t, D_MODEL), jnp.float32),
            jax.ShapeDtypeStruct((bsz, FFN_CONV_WIDTH - 1, 2 * D_FF), jnp.float32),
        ],
        scratch_shapes=[
            pltpu.VMEM((2 * FF_BLOCKS, FFN_HIST + tile, LANES), jnp.float32),
            pltpu.VMEM((tile, D_MODEL), jnp.bfloat16),
            pltpu.VMEM((tile, D_FF), jnp.bfloat16),
        ],
        compiler_params=pltpu.CompilerParams(
            dimension_semantics=("arbitrary", "arbitrary"), vmem_limit_bytes=VMEM_LIMIT),
        name="ffn_prompt",
    )(x, p["norm2_w"], p["w_up"], p["ffn_conv_w"], p["ffn_conv_b"], p["w_down"], p["norm_f_w"])


def _proj_sample_kernel(x_ref, n1_ref, win_ref, wgu_ref, bg_ref, cs_ref, cw_ref, cb_ref,
                        lnw_ref, lnb_ref,
                        qd_ref, k_ref, a_ref, v_ref, sc_ref, g_ref, glu_ref, ob_ref):
    x = x_ref[...]
    proj = _dot(_bf(_rms(x, n1_ref[...])), win_ref[...])
    g = _log_decay(proj[:, C_A:C_A + LANES], wgu_ref, bg_ref)
    q = proj[:, C_Q:C_K] * (GLA_DK ** -0.5)
    k = proj[:, C_K:C_V]
    q_dec = q * jnp.exp(g)
    k_inv = k * jnp.exp(-g)
    qd_ref[...] = q_dec
    k_ref[...] = k * jnp.exp(g - g)
    a_ref[...] = jnp.exp(g)
    v_ref[...] = proj[:, C_V:C_G]
    g_ref[...] = proj[:, C_G:IN_PAD]
    qk = q_dec * k_inv
    for hh in range(GLA_HEADS):
        sc_ref[:, hh * LANES:(hh + 1) * LANES] = jnp.broadcast_to(
            jnp.sum(qk[:, hh * GLA_DK:(hh + 1) * GLA_DK], axis=-1, keepdims=True),
            (x.shape[0], LANES))
    glu = proj[:, C_U:C_U + CONV_CH] * jax.nn.sigmoid(proj[:, C_U + CONV_CH:C_A])
    for j in range(CONV_WIDTH - 2):
        glu_ref[j] = cs_ref[j + 1]
    glu_ref[CONV_WIDTH - 2] = glu
    acc = cb_ref[...] + cw_ref[CONV_WIDTH - 1:CONV_WIDTH, :] * glu
    for j in range(CONV_WIDTH - 1):
        acc = acc + cw_ref[j:j + 1, :] * cs_ref[j]
    mu = jnp.mean(acc, axis=-1, keepdims=True)
    var = jnp.mean(jnp.square(acc - mu), axis=-1, keepdims=True)
    y = (acc - mu) * lax.rsqrt(var + EPS) * lnw_ref[...] + lnb_ref[...]
    ob_ref[...] = _silu(y)


def _proj_sample(x, cs_t, layer, p):
    n = x.shape[0]
    full = lambda shape: pl.BlockSpec(shape, lambda i: (0,) * len(shape))
    f32 = jnp.float32
    return pl.pallas_call(
        _proj_sample_kernel,
        grid=(1,),
        in_specs=[
            full((n, D_MODEL)),
            _const_spec((1, D_MODEL), layer),
            _const_spec((D_MODEL, IN_PAD), layer),
            _const_spec((LANES, GLA_KEY_WIDTH), layer),
            _const_spec((1, GLA_KEY_WIDTH), layer),
            pl.BlockSpec((None, CONV_WIDTH - 1, n, CONV_CH), lambda i: (layer, 0, 0, 0)),
            _const_spec((CONV_WIDTH, CONV_CH), layer),
            _const_spec((1, CONV_CH), layer),
            _const_spec((1, CONV_CH), layer),
            _const_spec((1, CONV_CH), layer),
        ],
        out_specs=[full((n, GLA_KEY_WIDTH)), full((n, GLA_KEY_WIDTH)), full((n, GLA_KEY_WIDTH)),
                   full((n, GLA_WIDTH)), full((n, GLA_WIDTH)), full((n, GLA_WIDTH)),
                   full((CONV_WIDTH - 1, n, CONV_CH)), full((n, CONV_CH))],
        out_shape=[jax.ShapeDtypeStruct((n, GLA_KEY_WIDTH), f32)] * 3
        + [jax.ShapeDtypeStruct((n, GLA_WIDTH), f32)] * 3
        + [jax.ShapeDtypeStruct((CONV_WIDTH - 1, n, CONV_CH), f32),
           jax.ShapeDtypeStruct((n, CONV_CH), f32)],
        compiler_params=pltpu.CompilerParams(
            dimension_semantics=("arbitrary",), vmem_limit_bytes=VMEM_LIMIT),
        name="proj_sample",
    )(x, p["norm1_w"], p["w_in"], p["w_gate_up"], p["b_gate"], cs_t, p["conv_w"], p["conv_b"],
      p["conv_ln_w"], p["conv_ln_b"])


def _gla_sample_kernel(qd_ref, k_ref, a_ref, v_ref, s_ref, o_ref, so_ref):
    rows = 2 * SUBLANES
    rid = lax.broadcasted_iota(jnp.int32, (rows, GLA_KEY_WIDTH), 0)
    lane = lax.broadcasted_iota(jnp.int32, (rows, GLA_KEY_WIDTH), 1)
    head_row = jnp.logical_and(lane >= rid * GLA_DK, lane < (rid + 1) * GLA_DK)
    vrid = lax.broadcasted_iota(jnp.int32, (rows, GLA_DV), 0)
    ones3 = _bf(jnp.where(vrid < 3, 1.0, 0.0))

    def body(bb, carry):
        s = s_ref[bb].reshape(GLA_HEADS * GLA_DK, GLA_DV)
        v = v_ref[bb]
        q_rows = _bf(jnp.where(head_row, qd_ref[bb], 0.0))
        o_rows = _dot(q_rows, _bf(s))
        o_ref[bb] = jnp.concatenate([o_rows[hh:hh + 1] for hh in range(GLA_HEADS)], axis=1)
        k_rows = _bf(jnp.where(head_row, k_ref[bb], 0.0))
        v_rows = _bf(jnp.concatenate(
            [v[:, hh * GLA_DV:(hh + 1) * GLA_DV] for hh in range(GLA_HEADS)]
            + [jnp.zeros((rows - GLA_HEADS, GLA_DV), jnp.float32)], axis=0))
        kv = lax.dot_general(k_rows, v_rows, (((0,), (0,)), ((), ())),
                             preferred_element_type=jnp.float32)
        a_hi, a_mid, a_lo = _split3(a_ref[bb])
        a_rows = jnp.concatenate(
            [a_hi, a_mid, a_lo, jnp.zeros((rows - 3, GLA_KEY_WIDTH), jnp.bfloat16)], axis=0)
        a_col = lax.dot_general(a_rows, ones3, (((0,), (0,)), ((), ())),
                                preferred_element_type=jnp.float32)
        so_ref[bb] = (a_col * s + kv).reshape(GLA_HEADS, GLA_DK, GLA_DV)
        return carry

    lax.fori_loop(0, SAMPLE_GLA_BLOCK, body, 0, unroll=SAMPLE_GLA_UNROLL)


def _gla_sample(qd, k, a, v, s_all, layer):
    n = qd.shape[0]
    blk = SAMPLE_GLA_BLOCK
    vec = lambda w: pl.BlockSpec((blk, 1, w), lambda i: (i, 0, 0))
    sspec = pl.BlockSpec((None, blk, GLA_HEADS, GLA_DK, GLA_DV), lambda i: (layer, i, 0, 0, 0))
    o, s_new = pl.pallas_call(
        _gla_sample_kernel,
        grid=(n // blk,),
        in_specs=[vec(GLA_KEY_WIDTH), vec(GLA_KEY_WIDTH), vec(GLA_KEY_WIDTH), vec(GLA_WIDTH), sspec],
        out_specs=[vec(GLA_WIDTH), sspec],
        out_shape=[jax.ShapeDtypeStruct((n, 1, GLA_WIDTH), jnp.float32),
                   jax.ShapeDtypeStruct(s_all.shape, jnp.float32)],
        input_output_aliases={4: 1},
        compiler_params=pltpu.CompilerParams(
            dimension_semantics=("arbitrary",), vmem_limit_bytes=VMEM_LIMIT),
        name="gla_sample",
    )(qd[:, None, :], k[:, None, :], a[:, None, :], v[:, None, :], s_all)
    return o[:, 0, :], s_new


def _tail_sample_kernel(x_ref, oi_ref, sc_ref, v_ref, g_ref, ob_ref, gnw_ref, wout_ref,
                        n2_ref, wup_ref, st0_ref, st1_ref, fw_ref, fb_ref, wdown_ref, nf_ref,
                        xo_ref, up_ref, *, final_norm):
    parts = []
    for hh in range(GLA_HEADS):
        cols = slice(hh * GLA_DV, (hh + 1) * GLA_DV)
        o = oi_ref[:, cols] + sc_ref[:, cols] * v_ref[:, cols]
        o = o * lax.rsqrt(jnp.mean(o * o, axis=-1, keepdims=True) + EPS)
        parts.append(_bf(o * gnw_ref[:, cols] * _silu(g_ref[:, cols])))
    parts.append(_bf(ob_ref[...]))
    x = x_ref[...] + _dot(jnp.concatenate(parts, axis=1), wout_ref[...])

    h2 = _bf(_rms(x, n2_ref[...]))
    up = _dot(h2, wup_ref[...])
    up_ref[0] = st1_ref[...]
    up_ref[1] = up
    upc = (fb_ref[...] + fw_ref[0:1, :] * st0_ref[...] + fw_ref[1:2, :] * st1_ref[...]
           + fw_ref[2:3, :] * up)
    gated = _bf(_silu(upc[:, :D_FF]) * upc[:, D_FF:])
    y = x + _dot(gated, wdown_ref[...])
    if final_norm:
        y = _rms(y, nf_ref[...])
    xo_ref[...] = y


def _tail_sample(x, oi, sc, v, g, ob, ffn_t, layer, p, final_norm):
    n = x.shape[0]
    full = lambda shape: pl.BlockSpec(shape, lambda i: (0,) * len(shape))
    plane = lambda j: pl.BlockSpec((None, None, n, 2 * D_FF), lambda i: (layer, j, 0, 0))
    return pl.pallas_call(
        functools.partial(_tail_sample_kernel, final_norm=final_norm),
        grid=(1,),
        in_specs=[
            full((n, D_MODEL)), full((n, GLA_WIDTH)), full((n, GLA_WIDTH)), full((n, GLA_WIDTH)),
            full((n, GLA_WIDTH)), full((n, CONV_CH)),
            _const_spec((1, GLA_WIDTH), layer),
            _const_spec((D_MODEL, D_MODEL), layer),
            _const_spec((1, D_MODEL), layer),
            _const_spec((D_MODEL, 2 * D_FF), layer),
            plane(0), plane(1),
            _const_spec((FFN_CONV_WIDTH, 2 * D_FF), layer),
            _const_spec((1, 2 * D_FF), layer),
            _const_spec((D_FF, D_MODEL), layer),
            pl.BlockSpec((1, D_MODEL), lambda i: (0, 0)),
        ],
        out_specs=[full((n, D_MODEL)), full((FFN_CONV_WIDTH - 1, n, 2 * D_FF))],
        out_shape=[jax.ShapeDtypeStruct((n, D_MODEL), jnp.float32),
                   jax.ShapeDtypeStruct((FFN_CONV_WIDTH - 1, n, 2 * D_FF), jnp.float32)],
        compiler_params=pltpu.CompilerParams(
            dimension_semantics=("arbitrary",), vmem_limit_bytes=VMEM_LIMIT),
        name="tail_sample",
    )(x, oi, sc, v, g, ob, p["gla_norm_w"], p["w_out"], p["norm2_w"], p["w_up"], ffn_t, ffn_t,
      p["ffn_conv_w"], p["ffn_conv_b"], p["w_down"], p["norm_f_w"])


W_IN_ROWS = 256


def _w_in_layout_kernel(w_ref, o_ref):
    a0 = 2 * GLA_KEY_WIDTH + 2 * GLA_WIDTH
    tail = w_ref[:, a0:]
    o_ref[:, C_U:C_A] = _bf(tail[:, GATE_RANK:])
    lane = lax.broadcasted_iota(jnp.int32, (W_IN_ROWS, LANES), 1)
    o_ref[:, C_A:C_Q] = _bf(jnp.where(lane < GATE_RANK, tail[:, 0:LANES], 0.0))
    o_ref[:, C_Q:IN_PAD] = _bf(w_ref[:, 0:a0])


def _w_in_layout(w_in):
    depth, d, cols = w_in.shape
    return pl.pallas_call(
        _w_in_layout_kernel,
        grid=(depth, d // W_IN_ROWS),
        in_specs=[pl.BlockSpec((None, W_IN_ROWS, cols), lambda l, r: (l, r, 0))],
        out_specs=pl.BlockSpec((None, W_IN_ROWS, IN_PAD), lambda l, r: (l, r, 0)),
        out_shape=jax.ShapeDtypeStruct((depth, d, IN_PAD), jnp.bfloat16),
        compiler_params=pltpu.CompilerParams(dimension_semantics=("arbitrary", "arbitrary")),
        name="w_in_layout",
    )(w_in)


def _prepare_params(norm1_w, w_in, w_gate_up, b_gate, gla_norm_w, conv_w, conv_b, conv_ln_w,
                    conv_ln_b, w_out, norm2_w, w_up, ffn_conv_w, ffn_conv_b, w_down, norm_f_w):
    depth = w_in.shape[0]
    wgu = jnp.concatenate(
        [w_gate_up, jnp.zeros((depth, LANES - GATE_RANK, GLA_KEY_WIDTH), w_gate_up.dtype)], axis=1)
    row = lambda a: a[:, None, :]
    return {
        "norm1_w": row(norm1_w), "w_in": _w_in_layout(w_in), "w_gate_up": _bf(wgu),
        "b_gate": row(b_gate),
        "gla_norm_w": row(gla_norm_w), "conv_w": conv_w, "conv_b": row(conv_b),
        "conv_ln_w": row(conv_ln_w), "conv_ln_b": row(conv_ln_b), "w_out": _bf(w_out),
        "norm2_w": row(norm2_w), "w_up": _bf(w_up), "ffn_conv_w": ffn_conv_w,
        "ffn_conv_b": row(ffn_conv_b), "w_down": _bf(w_down), "norm_f_w": norm_f_w[None, :],
    }


def kernel(x_prompt, x_sample, state_gla, state_conv, state_ffn, norm1_w, w_in, w_gate_up, b_gate,
           gla_norm_w, conv_w, conv_b, conv_ln_w, conv_ln_b, w_out, norm2_w, w_up, ffn_conv_w,
           ffn_conv_b, w_down, norm_f_w):
    p = _prepare_params(norm1_w, w_in, w_gate_up, b_gate, gla_norm_w, conv_w, conv_b, conv_ln_w,
                        conv_ln_b, w_out, norm2_w, w_up, ffn_conv_w, ffn_conv_b, w_down, norm_f_w)
    depth = w_in.shape[0]

    x = x_prompt
    gla_p, conv_p, ffn_p = [], [], []
    for l in range(depth):
        x, s_new, c_buf = _mixer_prompt(x, l, p)
        x, f_buf = _ffn_prompt(x, l, p, final_norm=(l == depth - 1))
        gla_p.append(s_new)
        conv_p.append(c_buf)
        ffn_p.append(f_buf)
    y_prompt = x

    xs = x_sample[:, 0, :]
    conv_t = jnp.swapaxes(state_conv, 1, 2)
    ffn_t = jnp.swapaxes(state_ffn, 1, 2)
    gla_s = state_gla
    glus, ups = [], []
    for l in range(depth):
        qd, k, a, v, sc, g, glu, ob = _proj_sample(xs, conv_t, l, p)
        oi, gla_s = _gla_sample(qd, k, a, v, gla_s, l)
        xs, up = _tail_sample(xs, oi, sc, v, g, ob, ffn_t, l, p, final_norm=(l == depth - 1))
        glus.append(glu)
        ups.append(up)
    y_sample = xs[:, None, :]
    conv_s = jnp.swapaxes(jnp.stack(glus), 1, 2)
    ffn_s = jnp.swapaxes(jnp.stack(ups), 1, 2)

    return (y_prompt, y_sample, jnp.stack(gla_p), gla_s, jnp.stack(conv_p), conv_s,
            jnp.stack(ffn_p), ffn_s)
```

```python
import functools

import jax
import jax.numpy as jnp
from jax import lax
from jax.experimental import pallas as pl
from jax.experimental.pallas import tpu as pltpu

D_MODEL = 1024
GLA_WIDTH = 512
CONV_CH = 512
GLA_HEADS = 4
GLA_DV = 128
GLA_KEY_WIDTH = 256
GLA_DK = 64
GATE_RANK = 16
GATE_TAU = 16.0
GLA_CHUNK = 64
CONV_WIDTH = 31
FFN_CONV_WIDTH = 3
D_FF = 2816
EPS = 1e-6

LANES = 128
SUBLANES = 8

CONV_BLOCKS = CONV_CH // LANES
C_U = 0
C_A = C_U + 2 * CONV_CH
C_Q = C_A + LANES
C_K = C_Q + GLA_KEY_WIDTH
C_V = C_K + GLA_KEY_WIDTH
C_G = C_V + GLA_WIDTH
IN_PAD = C_G + GLA_WIDTH

CONV_HIST = 32
FFN_HIST = SUBLANES
MIXER_TILE = 512
MIXER_PASS = 512
FFN_TILE = 256
CUMSUM_ROWS = 256
CONV_ROWS = 64
SAMPLE_GLA_BLOCK = 16
SAMPLE_GLA_UNROLL = 4

VMEM_LIMIT = 56 * 1024 * 1024


def _rms(x, w):
    return x * lax.rsqrt(jnp.mean(x * x, axis=-1, keepdims=True) + EPS) * w


def _log_sigmoid(z):
    return jnp.minimum(z, 0.0) - jnp.log1p(jnp.exp(-jnp.abs(z)))


def _silu(x):
    return x * jax.nn.sigmoid(x)


def _bf(x):
    return x.astype(jnp.bfloat16)


def _dot(a, b):
    return jnp.dot(a, b, preferred_element_type=jnp.float32)


def _dot_nt(a, b):
    return lax.dot_general(a, b, (((1,), (1,)), ((), ())), preferred_element_type=jnp.float32)


def _split3(x):
    hi = _bf(x)
    r = x - hi.astype(jnp.float32)
    mid = _bf(r)
    lo = _bf(r - mid.astype(jnp.float32))
    return hi, mid, lo


def _log_decay(alow, wgu_ref, bg_ref):
    z = _dot(_bf(alow), wgu_ref[...]) + bg_ref[...]
    return _log_sigmoid(z) * (1.0 / GATE_TAU)


def _const_spec(shape, layer):
    nd = len(shape)
    return pl.BlockSpec((None,) + tuple(shape), lambda *_: (layer,) + (0,) * nd,
                        pipeline_mode=pl.Buffered(1))


CONV_FIRST = CONV_HIST - (CONV_WIDTH - 1)


def _conv_block(cb, xp_sc, cv_sc, cw_ref, cb_ref, base, rows):
    cols = slice(cb * LANES, (cb + 1) * LANES)
    for r0 in range(base, base + rows, CONV_ROWS):
        acc = jnp.broadcast_to(cb_ref[:, cols], (CONV_ROWS, LANES))
        for j in range(CONV_WIDTH):
            s = r0 + CONV_FIRST + j
            acc = acc + cw_ref[j:j + 1, cols] * xp_sc[cb, s:s + CONV_ROWS, :]
        cv_sc[cb, r0:r0 + CONV_ROWS, :] = acc


def _gla_rows(aqk_sc, v_sc, g_sc, wgu_ref, bg_ref, gnw_ref, st_sc, oab_sc, base, rows):
    q0 = C_Q - C_A
    k0 = C_K - C_A
    tile = rows
    n_ch = tile // GLA_CHUNK
    span = min(tile, CUMSUM_ROWS)
    win = slice(base, base + rows)
    trow = lax.broadcasted_iota(jnp.int32, (span, span), 0)
    tcol = lax.broadcasted_iota(jnp.int32, (span, span), 1)
    same_block = jnp.bitwise_and(trow, -GLA_CHUNK) == jnp.bitwise_and(tcol, -GLA_CHUNK)
    tril = _bf(jnp.where(jnp.logical_and(same_block, tcol <= trow), 1.0, 0.0))
    srow = lax.broadcasted_iota(jnp.int32, (GLA_HEADS * GLA_CHUNK, GLA_CHUNK), 0)
    scol = lax.broadcasted_iota(jnp.int32, (GLA_HEADS * GLA_CHUNK, GLA_CHUNK), 1)
    causal = scol <= jnp.bitwise_and(srow, GLA_CHUNK - 1)
    lane = lax.broadcasted_iota(jnp.int32, (1, GLA_KEY_WIDTH), 1)
    head_masks = [jnp.logical_and(lane >= hh * GLA_DK, lane < (hh + 1) * GLA_DK)
                  for hh in range(GLA_HEADS)]
    blocks = [slice(c * GLA_CHUNK, (c + 1) * GLA_CHUNK) for c in range(n_ch)]

    def stack_heads(a):
        return _bf(jnp.concatenate([jnp.where(m, a, 0.0) for m in head_masks], axis=0))

    g = _log_decay(aqk_sc[win, 0:LANES], wgu_ref, bg_ref)
    ghi, gmid, glo = _split3(g)
    b = jnp.concatenate(
        [_dot(tril, ghi[r0:r0 + span]) + _dot(tril, gmid[r0:r0 + span]) + _dot(tril, glo[r0:r0 + span])
         for r0 in range(0, tile, span)], axis=0)
    last = [b[r.stop - 1:r.stop, :] for r in blocks]
    b_last = jnp.concatenate([jnp.broadcast_to(l, (GLA_CHUNK, GLA_KEY_WIDTH)) for l in last], axis=0)
    q = aqk_sc[win, q0:k0] * (GLA_DK ** -0.5)
    k = aqk_sc[win, k0:k0 + GLA_KEY_WIDTH]
    q_dec = q * jnp.exp(b)
    k_inv = _bf(k * jnp.exp(-b))
    k_rem = k * jnp.exp(b_last - b)

    q_stack = [stack_heads(q_dec[r]) for r in blocks]
    k_stack = [stack_heads(k_rem[r]) for r in blocks]
    scores = [jnp.where(causal, _dot_nt(q_stack[c], k_inv[blocks[c]]), 0.0) for c in range(n_ch)]
    tile_rows = [slice(base + r.start, base + r.stop) for r in blocks]
    v_blk = [v_sc[r, :] for r in tile_rows]
    v_heads = [[vb[:, hh * GLA_DV:(hh + 1) * GLA_DV] for hh in range(GLA_HEADS)] for vb in v_blk]
    kv = [_dot(_bf(jnp.concatenate(v_heads[c], axis=0).T), k_stack[c]) for c in range(n_ch)]
    o_intra = [jnp.concatenate(
        [_dot(_bf(scores[c][hh * GLA_CHUNK:(hh + 1) * GLA_CHUNK]), _bf(v_heads[c][hh]))
         for hh in range(GLA_HEADS)], axis=0) for c in range(n_ch)]

    st = st_sc[...]
    states = []
    for c in range(n_ch):
        states.append(_bf(st))
        st = jnp.exp(last[c]) * st + kv[c]
    st_sc[...] = st

    for c in range(n_ch):
        o = _dot_nt(q_stack[c], states[c]) + o_intra[c]
        o = o * lax.rsqrt(jnp.mean(o * o, axis=-1, keepdims=True) + EPS)
        for hh in range(GLA_HEADS):
            cols = slice(hh * GLA_DV, (hh + 1) * GLA_DV)
            oh = o[hh * GLA_CHUNK:(hh + 1) * GLA_CHUNK] * gnw_ref[:, cols]
            gate = g_sc[tile_rows[c], cols]
            oab_sc[tile_rows[c], cols] = _bf(oh * _silu(gate))


def _mixer_prompt_kernel(x_ref, n1_ref, win_ref, wgu_ref, bg_ref, gnw_ref, cw_ref, cb_ref,
                         lnw_ref, lnb_ref, wout_ref,
                         xo_ref, gla_ref, conv_ref,
                         aqk_sc, v_sc, g_sc, st_sc, xp_sc, oab_sc, h_sc, cv_sc, *, tile, n_tiles):
    i = pl.program_id(1)

    @pl.when(i == 0)
    def _():
        st_sc[...] = jnp.zeros_like(st_sc)
        xp_sc[:, 0:CONV_HIST, :] = jnp.zeros((CONV_BLOCKS, CONV_HIST, LANES), jnp.float32)

    h_sc[...] = _bf(_rms(x_ref[...], n1_ref[...]))
    rows = min(tile, MIXER_PASS)

    def project(c0, c1):
        return _dot(h_sc[...], win_ref[:, c0:c1])

    u = project(C_U, C_A)
    for cb in range(CONV_BLOCKS):
        c0 = cb * LANES
        xp_sc[cb, CONV_HIST:CONV_HIST + tile, :] = (
            u[:, c0:c0 + LANES] * jax.nn.sigmoid(u[:, CONV_CH + c0:CONV_CH + c0 + LANES]))
    aqk_sc[...] = project(C_A, C_V)
    v_sc[...] = project(C_V, C_G)
    g_sc[...] = project(C_G, IN_PAD)

    def conv(cb, base):
        _conv_block(cb, xp_sc, cv_sc, cw_ref, cb_ref, base, rows)

    def layer_norm_swish(base):
        win = slice(base, base + rows)
        cv = jnp.concatenate([cv_sc[cb, win, :] for cb in range(CONV_BLOCKS)], axis=1)
        mu = jnp.mean(cv, axis=-1, keepdims=True)
        var = jnp.mean(jnp.square(cv - mu), axis=-1, keepdims=True)
        y = (cv - mu) * lax.rsqrt(var + EPS) * lnw_ref[...] + lnb_ref[...]
        oab_sc[win, GLA_WIDTH:GLA_WIDTH + CONV_CH] = _bf(_silu(y))

    def finish(base):
        win = slice(base, base + rows)
        xo_ref[win, :] = x_ref[win, :] + _dot(oab_sc[win, :], wout_ref[...])

    for base in range(0, tile, rows):
        for cb in range(CONV_BLOCKS):
            conv(cb, base)
        layer_norm_swish(base)
        _gla_rows(aqk_sc, v_sc, g_sc, wgu_ref, bg_ref, gnw_ref, st_sc, oab_sc, base, rows)
        finish(base)
    xp_sc[:, 0:CONV_HIST, :] = xp_sc[:, tile:tile + CONV_HIST, :]

    @pl.when(i == n_tiles - 1)
    def _():
        gla_ref[...] = st_sc[...].T.reshape(GLA_HEADS, GLA_DK, GLA_DV)
        for cb in range(CONV_BLOCKS):
            conv_ref[:, cb * LANES:(cb + 1) * LANES] = xp_sc[cb, CONV_FIRST:CONV_HIST, :]


def _mixer_prompt(x, layer, p):
    bsz, t, _ = x.shape
    tile = MIXER_TILE
    n_tiles = t // tile
    kern = functools.partial(_mixer_prompt_kernel, tile=tile, n_tiles=n_tiles)
    return pl.pallas_call(
        kern,
        grid=(bsz, n_tiles),
        in_specs=[
            pl.BlockSpec((None, tile, D_MODEL), lambda b, i: (b, i, 0)),
            _const_spec((1, D_MODEL), layer),
            _const_spec((D_MODEL, IN_PAD), layer),
            _const_spec((LANES, GLA_KEY_WIDTH), layer),
            _const_spec((1, GLA_KEY_WIDTH), layer),
            _const_spec((1, GLA_WIDTH), layer),
            _const_spec((CONV_WIDTH, CONV_CH), layer),
            _const_spec((1, CONV_CH), layer),
            _const_spec((1, CONV_CH), layer),
            _const_spec((1, CONV_CH), layer),
            _const_spec((D_MODEL, D_MODEL), layer),
        ],
        out_specs=[
            pl.BlockSpec((None, tile, D_MODEL), lambda b, i: (b, i, 0)),
            pl.BlockSpec((None, GLA_HEADS, GLA_DK, GLA_DV), lambda b, i: (b, 0, 0, 0)),
            pl.BlockSpec((None, CONV_WIDTH - 1, CONV_CH), lambda b, i: (b, 0, 0)),
        ],
        out_shape=[
            jax.ShapeDtypeStruct((bsz, t, D_MODEL), jnp.float32),
            jax.ShapeDtypeStruct((bsz, GLA_HEADS, GLA_DK, GLA_DV), jnp.float32),
            jax.ShapeDtypeStruct((bsz, CONV_WIDTH - 1, CONV_CH), jnp.float32),
        ],
        scratch_shapes=[
            pltpu.VMEM((tile, C_V - C_A), jnp.float32),
            pltpu.VMEM((tile, GLA_WIDTH), jnp.float32),
            pltpu.VMEM((tile, GLA_WIDTH), jnp.float32),
            pltpu.VMEM((GLA_DV, GLA_KEY_WIDTH), jnp.float32),
            pltpu.VMEM((CONV_BLOCKS, CONV_HIST + tile, LANES), jnp.float32),
            pltpu.VMEM((tile, D_MODEL), jnp.bfloat16),
            pltpu.VMEM((tile, D_MODEL), jnp.bfloat16),
            pltpu.VMEM((CONV_BLOCKS, tile, LANES), jnp.float32),
        ],
        compiler_params=pltpu.CompilerParams(
            dimension_semantics=("arbitrary", "arbitrary"), vmem_limit_bytes=VMEM_LIMIT),
        name="mixer_prompt",
    )(x, p["norm1_w"], p["w_in"], p["w_gate_up"], p["b_gate"], p["gla_norm_w"], p["conv_w"],
      p["conv_b"], p["conv_ln_w"], p["conv_ln_b"], p["w_out"])


FF_BLOCKS = D_FF // LANES


FF_GROUP = 2


def _ffn_gate(up_r, cb, fw_ref, fb_ref, tile):
    def conv(blk):
        cols = slice(blk * LANES, (blk + 1) * LANES)
        acc = jnp.broadcast_to(fb_ref[:, cols], (tile, LANES))
        for j in range(FFN_CONV_WIDTH):
            r = FFN_HIST - (FFN_CONV_WIDTH - 1) + j
            acc = acc + fw_ref[j:j + 1, cols] * up_r[blk, r:r + tile, :]
        return acc

    return _bf(_silu(conv(cb)) * conv(FF_BLOCKS + cb))


def _ffn_prompt_kernel(x_ref, n2_ref, wup_ref, fw_ref, fb_ref, wdown_ref, nf_ref,
                       xo_ref, ffn_ref, up_sc, h_sc, gated_sc, *, tile, n_tiles, final_norm):
    i = pl.program_id(1)

    @pl.when(i == 0)
    def _():
        up_sc[:, 0:FFN_HIST, :] = jnp.zeros((2 * FF_BLOCKS, FFN_HIST, LANES), jnp.float32)

    x = x_ref[...]
    h_sc[...] = _bf(_rms(x, n2_ref[...]))
    xo_ref[...] = x
    width = FF_GROUP * LANES
    n_groups = FF_BLOCKS // FF_GROUP

    def up_group(grp):
        for half in range(2):
            c0 = half * D_FF + grp * width
            up = _dot(h_sc[...], wup_ref[:, c0:c0 + width])
            for j in range(FF_GROUP):
                up_sc[half * FF_BLOCKS + grp * FF_GROUP + j, FFN_HIST:FFN_HIST + tile, :] = (
                    up[:, j * LANES:(j + 1) * LANES])

    def gate_group(grp):
        for j in range(FF_GROUP):
            cb = grp * FF_GROUP + j
            gated_sc[:, cb * LANES:(cb + 1) * LANES] = _ffn_gate(up_sc, cb, fw_ref, fb_ref, tile)

    def down_group(grp):
        rows = slice(grp * width, (grp + 1) * width)
        xo_ref[...] += _dot(gated_sc[:, rows], wdown_ref[rows, :])

    for step in range(n_groups + 2):
        if step < n_groups:
            up_group(step)
        if 1 <= step <= n_groups:
            gate_group(step - 1)
        if step >= 2:
            down_group(step - 2)
    if final_norm:
        xo_ref[...] = _rms(xo_ref[...], nf_ref[...])

    @pl.when(i == n_tiles - 1)
    def _():
        for cb in range(2 * FF_BLOCKS):
            ffn_ref[:, cb * LANES:(cb + 1) * LANES] = up_sc[
                cb, FFN_HIST + tile - (FFN_CONV_WIDTH - 1):FFN_HIST + tile, :]

    up_sc[:, 0:FFN_HIST, :] = up_sc[:, tile:tile + FFN_HIST, :]


def _ffn_prompt(x, layer, p, final_norm):
    bsz, t, _ = x.shape
    tile = FFN_TILE
    n_tiles = t // tile
    kern = functools.partial(_ffn_prompt_kernel, tile=tile, n_tiles=n_tiles, final_norm=final_norm)
    return pl.pallas_call(
        kern,
        grid=(bsz, n_tiles),
        in_specs=[
            pl.BlockSpec((None, tile, D_MODEL), lambda b, i: (b, i, 0)),
            _const_spec((1, D_MODEL), layer),
            _const_spec((D_MODEL, 2 * D_FF), layer),
            _const_spec((FFN_CONV_WIDTH, 2 * D_FF), layer),
            _const_spec((1, 2 * D_FF), layer),
            _const_spec((D_FF, D_MODEL), layer),
            pl.BlockSpec((1, D_MODEL), lambda b, i: (0, 0), pipeline_mode=pl.Buffered(1)),
        ],
        out_specs=[
            pl.BlockSpec((None, tile, D_MODEL), lambda b, i: (b, i, 0)),
            pl.BlockSpec((None, FFN_CONV_WIDTH - 1, 2 * D_FF), lambda b, i: (b, 0, 0)),
        ],
        out_shape=[
            jax.ShapeDtypeStruct((bsz, t, D_MODEL), jnp.float32),
            jax.ShapeDtypeStruct((bsz, FFN_CONV_WIDTH - 1, 2 * D_FF), jnp.float32),
        ],
        scratch_shapes=[
            pltpu.VMEM((2 * FF_BLOCKS, FFN_HIST + tile, LANES), jnp.float32),
            pltpu.VMEM((tile, D_MODEL), jnp.bfloat16),
            pltpu.VMEM((tile, D_FF), jnp.bfloat16),
        ],
        compiler_params=pltpu.CompilerParams(
            dimension_semantics=("arbitrary", "arbitrary"), vmem_limit_bytes=VMEM_LIMIT),
        name="ffn_prompt",
    )(x, p["norm2_w"], p["w_up"], p["ffn_conv_w"], p["ffn_conv_b"], p["w_down"], p["norm_f_w"])


def _proj_sample_kernel(x_ref, n1_ref, win_ref, wgu_ref, bg_ref, cs_ref, cw_ref, cb_ref,
                        lnw_ref, lnb_ref,
                        qd_ref, k_ref, a_ref, v_ref, sc_ref, g_ref, glu_ref, ob_ref):
    x = x_ref[...]
    proj = _dot(_bf(_rms(x, n1_ref[...])), win_ref[...])
    g = _log_decay(proj[:, C_A:C_A + LANES], wgu_ref, bg_ref)
    q = proj[:, C_Q:C_K] * (GLA_DK ** -0.5)
    k = proj[:, C_K:C_V]
    q_dec = q * jnp.exp(g)
    k_inv = k * jnp.exp(-g)
    qd_ref[...] = q_dec
    k_ref[...] = k * jnp.exp(g - g)
    a_ref[...] = jnp.exp(g)
    v_ref[...] = proj[:, C_V:C_G]
    g_ref[...] = proj[:, C_G:IN_PAD]
    qk = q_dec * k_inv
    for hh in range(GLA_HEADS):
        sc_ref[:, hh * LANES:(hh + 1) * LANES] = jnp.broadcast_to(
            jnp.sum(qk[:, hh * GLA_DK:(hh + 1) * GLA_DK], axis=-1, keepdims=True),
            (x.shape[0], LANES))
    glu = proj[:, C_U:C_U + CONV_CH] * jax.nn.sigmoid(proj[:, C_U + CONV_CH:C_A])
    for j in range(CONV_WIDTH - 2):
        glu_ref[j] = cs_ref[j + 1]
    glu_ref[CONV_WIDTH - 2] = glu
    acc = cb_ref[...] + cw_ref[CONV_WIDTH - 1:CONV_WIDTH, :] * glu
    for j in range(CONV_WIDTH - 1):
        acc = acc + cw_ref[j:j + 1, :] * cs_ref[j]
    mu = jnp.mean(acc, axis=-1, keepdims=True)
    var = jnp.mean(jnp.square(acc - mu), axis=-1, keepdims=True)
    y = (acc - mu) * lax.rsqrt(var + EPS) * lnw_ref[...] + lnb_ref[...]
    ob_ref[...] = _silu(y)


def _proj_sample(x, cs_t, layer, p):
    n = x.shape[0]
    full = lambda shape: pl.BlockSpec(shape, lambda i: (0,) * len(shape))
    f32 = jnp.float32
    return pl.pallas_call(
        _proj_sample_kernel,
        grid=(1,),
        in_specs=[
            full((n, D_MODEL)),
            _const_spec((1, D_MODEL), layer),
            _const_spec((D_MODEL, IN_PAD), layer),
            _const_spec((LANES, GLA_KEY_WIDTH), layer),
            _const_spec((1, GLA_KEY_WIDTH), layer),
            pl.BlockSpec((None, CONV_WIDTH - 1, n, CONV_CH), lambda i: (layer, 0, 0, 0)),
            _const_spec((CONV_WIDTH, CONV_CH), layer),
            _const_spec((1, CONV_CH), layer),
            _const_spec((1, CONV_CH), layer),
            _const_spec((1, CONV_CH), layer),
        ],
        out_specs=[full((n, GLA_KEY_WIDTH)), full((n, GLA_KEY_WIDTH)), full((n, GLA_KEY_WIDTH)),
                   full((n, GLA_WIDTH)), full((n, GLA_WIDTH)), full((n, GLA_WIDTH)),
                   full((CONV_WIDTH - 1, n, CONV_CH)), full((n, CONV_CH))],
        out_shape=[jax.ShapeDtypeStruct((n, GLA_KEY_WIDTH), f32)] * 3
        + [jax.ShapeDtypeStruct((n, GLA_WIDTH), f32)] * 3
        + [jax.ShapeDtypeStruct((CONV_WIDTH - 1, n, CONV_CH), f32),
           jax.ShapeDtypeStruct((n, CONV_CH), f32)],
        compiler_params=pltpu.CompilerParams(
            dimension_semantics=("arbitrary",), vmem_limit_bytes=VMEM_LIMIT),
        name="proj_sample",
    )(x, p["norm1_w"], p["w_in"], p["w_gate_up"], p["b_gate"], cs_t, p["conv_w"], p["conv_b"],
      p["conv_ln_w"], p["conv_ln_b"])


GLA_ROWS = 2 * SUBLANES


def _head_row_mask():
    rid = lax.broadcasted_iota(jnp.int32, (GLA_ROWS, GLA_KEY_WIDTH), 0)
    lane = lax.broadcasted_iota(jnp.int32, (GLA_ROWS, GLA_KEY_WIDTH), 1)
    return jnp.logical_and(lane >= rid * GLA_DK, lane < (rid + 1) * GLA_DK)


def _gla_out_sample_kernel(qd_ref, s_ref, o_ref):
    head_row = _head_row_mask()

    def body(bb, carry):
        s = s_ref[bb].reshape(GLA_HEADS * GLA_DK, GLA_DV)
        q_rows = _bf(jnp.where(head_row, qd_ref[bb], 0.0))
        o_rows = _dot(q_rows, _bf(s))
        o_ref[bb] = jnp.concatenate([o_rows[hh:hh + 1] for hh in range(GLA_HEADS)], axis=1)
        return carry

    lax.fori_loop(0, SAMPLE_GLA_BLOCK, body, 0, unroll=SAMPLE_GLA_UNROLL)


def _gla_update_sample_kernel(k_ref, a_ref, v_ref, s_ref, so_ref):
    rows = GLA_ROWS
    head_row = _head_row_mask()
    vrid = lax.broadcasted_iota(jnp.int32, (rows, GLA_DV), 0)
    ones3 = _bf(jnp.where(vrid < 3, 1.0, 0.0))

    def body(bb, carry):
        s = s_ref[bb].reshape(GLA_HEADS * GLA_DK, GLA_DV)
        v = v_ref[bb]
        k_rows = _bf(jnp.where(head_row, k_ref[bb], 0.0))
        v_rows = _bf(jnp.concatenate(
            [v[:, hh * GLA_DV:(hh + 1) * GLA_DV] for hh in range(GLA_HEADS)]
            + [jnp.zeros((rows - GLA_HEADS, GLA_DV), jnp.float32)], axis=0))
        kv = lax.dot_general(k_rows, v_rows, (((0,), (0,)), ((), ())),
                             preferred_element_type=jnp.float32)
        a_hi, a_mid, a_lo = _split3(a_ref[bb])
        a_rows = jnp.concatenate(
            [a_hi, a_mid, a_lo, jnp.zeros((rows - 3, GLA_KEY_WIDTH), jnp.bfloat16)], axis=0)
        a_col = lax.dot_general(a_rows, ones3, (((0,), (0,)), ((), ())),
                                preferred_element_type=jnp.float32)
        so_ref[bb] = (a_col * s + kv).reshape(GLA_HEADS, GLA_DK, GLA_DV)
        return carry

    lax.fori_loop(0, SAMPLE_GLA_BLOCK, body, 0, unroll=SAMPLE_GLA_UNROLL)


def _gla_out_sample(qd, s_all, layer):
    n = qd.shape[0]
    blk = SAMPLE_GLA_BLOCK
    vec = lambda w: pl.BlockSpec((blk, 1, w), lambda i: (i, 0, 0))
    sspec = pl.BlockSpec((None, blk, GLA_HEADS, GLA_DK, GLA_DV), lambda i: (layer, i, 0, 0, 0))
    o = pl.pallas_call(
        _gla_out_sample_kernel,
        grid=(n // blk,),
        in_specs=[vec(GLA_KEY_WIDTH), sspec],
        out_specs=vec(GLA_WIDTH),
        out_shape=jax.ShapeDtypeStruct((n, 1, GLA_WIDTH), jnp.float32),
        compiler_params=pltpu.CompilerParams(
            dimension_semantics=("arbitrary",), vmem_limit_bytes=VMEM_LIMIT),
        name="gla_out_sample",
    )(qd[:, None, :], s_all)
    return o[:, 0, :]


def _gla_update_sample(k_all, a_all, v_all, s_all):
    depth, n = k_all.shape[0], k_all.shape[1]
    blk = SAMPLE_GLA_BLOCK
    vec = lambda w: pl.BlockSpec((None, blk, 1, w), lambda l, i: (l, i, 0, 0))
    sspec = pl.BlockSpec((None, blk, GLA_HEADS, GLA_DK, GLA_DV), lambda l, i: (l, i, 0, 0, 0))
    return pl.pallas_call(
        _gla_update_sample_kernel,
        grid=(depth, n // blk),
        in_specs=[vec(GLA_KEY_WIDTH), vec(GLA_KEY_WIDTH), vec(GLA_WIDTH), sspec],
        out_specs=sspec,
        out_shape=jax.ShapeDtypeStruct(s_all.shape, jnp.float32),
        compiler_params=pltpu.CompilerParams(
            dimension_semantics=("arbitrary", "arbitrary"), vmem_limit_bytes=VMEM_LIMIT),
        name="gla_update_sample",
    )(k_all[:, :, None, :], a_all[:, :, None, :], v_all[:, :, None, :], s_all)


def _tail_sample_kernel(x_ref, oi_ref, sc_ref, v_ref, g_ref, ob_ref, gnw_ref, wout_ref,
                        n2_ref, wup_ref, st0_ref, st1_ref, fw_ref, fb_ref, wdown_ref, nf_ref,
                        xo_ref, up_ref, *, final_norm):
    parts = []
    for hh in range(GLA_HEADS):
        cols = slice(hh * GLA_DV, (hh + 1) * GLA_DV)
        o = oi_ref[:, cols] + sc_ref[:, cols] * v_ref[:, cols]
        o = o * lax.rsqrt(jnp.mean(o * o, axis=-1, keepdims=True) + EPS)
        parts.append(_bf(o * gnw_ref[:, cols] * _silu(g_ref[:, cols])))
    parts.append(_bf(ob_ref[...]))
    x = x_ref[...] + _dot(jnp.concatenate(parts, axis=1), wout_ref[...])

    h2 = _bf(_rms(x, n2_ref[...]))
    up = _dot(h2, wup_ref[...])
    up_ref[0] = st1_ref[...]
    up_ref[1] = up
    upc = (fb_ref[...] + fw_ref[0:1, :] * st0_ref[...] + fw_ref[1:2, :] * st1_ref[...]
           + fw_ref[2:3, :] * up)
    gated = _bf(_silu(upc[:, :D_FF]) * upc[:, D_FF:])
    y = x + _dot(gated, wdown_ref[...])
    if final_norm:
        y = _rms(y, nf_ref[...])
    xo_ref[...] = y


def _tail_sample(x, oi, sc, v, g, ob, ffn_t, layer, p, final_norm):
    n = x.shape[0]
    full = lambda shape: pl.BlockSpec(shape, lambda i: (0,) * len(shape))
    plane = lambda j: pl.BlockSpec((None, None, n, 2 * D_FF), lambda i: (layer, j, 0, 0))
    return pl.pallas_call(
        functools.partial(_tail_sample_kernel, final_norm=final_norm),
        grid=(1,),
        in_specs=[
            full((n, D_MODEL)), full((n, GLA_WIDTH)), full((n, GLA_WIDTH)), full((n, GLA_WIDTH)),
            full((n, GLA_WIDTH)), full((n, CONV_CH)),
            _const_spec((1, GLA_WIDTH), layer),
            _const_spec((D_MODEL, D_MODEL), layer),
            _const_spec((1, D_MODEL), layer),
            _const_spec((D_MODEL, 2 * D_FF), layer),
            plane(0), plane(1),
            _const_spec((FFN_CONV_WIDTH, 2 * D_FF), layer),
            _const_spec((1, 2 * D_FF), layer),
            _const_spec((D_FF, D_MODEL), layer),
            pl.BlockSpec((1, D_MODEL), lambda i: (0, 0)),
        ],
        out_specs=[full((n, D_MODEL)), full((FFN_CONV_WIDTH - 1, n, 2 * D_FF))],
        out_shape=[jax.ShapeDtypeStruct((n, D_MODEL), jnp.float32),
                   jax.ShapeDtypeStruct((FFN_CONV_WIDTH - 1, n, 2 * D_FF), jnp.float32)],
        compiler_params=pltpu.CompilerParams(
            dimension_semantics=("arbitrary",), vmem_limit_bytes=VMEM_LIMIT),
        name="tail_sample",
    )(x, oi, sc, v, g, ob, p["gla_norm_w"], p["w_out"], p["norm2_w"], p["w_up"], ffn_t, ffn_t,
      p["ffn_conv_w"], p["ffn_conv_b"], p["w_down"], p["norm_f_w"])


W_IN_ROWS = 256


def _w_in_layout_kernel(w_ref, o_ref):
    a0 = 2 * GLA_KEY_WIDTH + 2 * GLA_WIDTH
    tail = w_ref[:, a0:]
    o_ref[:, C_U:C_A] = _bf(tail[:, GATE_RANK:])
    lane = lax.broadcasted_iota(jnp.int32, (W_IN_ROWS, LANES), 1)
    o_ref[:, C_A:C_Q] = _bf(jnp.where(lane < GATE_RANK, tail[:, 0:LANES], 0.0))
    o_ref[:, C_Q:IN_PAD] = _bf(w_ref[:, 0:a0])


def _w_in_layout(w_in):
    depth, d, cols = w_in.shape
    return pl.pallas_call(
        _w_in_layout_kernel,
        grid=(depth, d // W_IN_ROWS),
        in_specs=[pl.BlockSpec((None, W_IN_ROWS, cols), lambda l, r: (l, r, 0))],
        out_specs=pl.BlockSpec((None, W_IN_ROWS, IN_PAD), lambda l, r: (l, r, 0)),
        out_shape=jax.ShapeDtypeStruct((depth, d, IN_PAD), jnp.bfloat16),
        compiler_params=pltpu.CompilerParams(dimension_semantics=("arbitrary", "arbitrary")),
        name="w_in_layout",
    )(w_in)


def _prepare_params(norm1_w, w_in, w_gate_up, b_gate, gla_norm_w, conv_w, conv_b, conv_ln_w,
                    conv_ln_b, w_out, norm2_w, w_up, ffn_conv_w, ffn_conv_b, w_down, norm_f_w):
    depth = w_in.shape[0]
    wgu = jnp.concatenate(
        [w_gate_up, jnp.zeros((depth, LANES - GATE_RANK, GLA_KEY_WIDTH), w_gate_up.dtype)], axis=1)
    row = lambda a: a[:, None, :]
    return {
        "norm1_w": row(norm1_w), "w_in": _w_in_layout(w_in), "w_gate_up": _bf(wgu),
        "b_gate": row(b_gate),
        "gla_norm_w": row(gla_norm_w), "conv_w": conv_w, "conv_b": row(conv_b),
        "conv_ln_w": row(conv_ln_w), "conv_ln_b": row(conv_ln_b), "w_out": _bf(w_out),
        "norm2_w": row(norm2_w), "w_up": _bf(w_up), "ffn_conv_w": ffn_conv_w,
        "ffn_conv_b": row(ffn_conv_b), "w_down": _bf(w_down), "norm_f_w": norm_f_w[None, :],
    }


def kernel(x_prompt, x_sample, state_gla, state_conv, state_ffn, norm1_w, w_in, w_gate_up, b_gate,
           gla_norm_w, conv_w, conv_b, conv_ln_w, conv_ln_b, w_out, norm2_w, w_up, ffn_conv_w,
           ffn_conv_b, w_down, norm_f_w):
    p = _prepare_params(norm1_w, w_in, w_gate_up, b_gate, gla_norm_w, conv_w, conv_b, conv_ln_w,
                        conv_ln_b, w_out, norm2_w, w_up, ffn_conv_w, ffn_conv_b, w_down, norm_f_w)
    depth = w_in.shape[0]

    x = x_prompt
    gla_p, conv_p, ffn_p = [], [], []
    for l in range(depth):
        x, s_new, c_buf = _mixer_prompt(x, l, p)
        x, f_buf = _ffn_prompt(x, l, p, final_norm=(l == depth - 1))
        gla_p.append(s_new)
        conv_p.append(c_buf)
        ffn_p.append(f_buf)
    y_prompt = x

    xs = x_sample[:, 0, :]
    conv_t = jnp.swapaxes(state_conv, 1, 2)
    ffn_t = jnp.swapaxes(state_ffn, 1, 2)
    glus, ups, ks, decays, vs = [], [], [], [], []
    for l in range(depth):
        qd, k, a, v, sc, g, glu, ob = _proj_sample(xs, conv_t, l, p)
        oi = _gla_out_sample(qd, state_gla, l)
        xs, up = _tail_sample(xs, oi, sc, v, g, ob, ffn_t, l, p, final_norm=(l == depth - 1))
        glus.append(glu)
        ups.append(up)
        ks.append(k)
        decays.append(a)
        vs.append(v)
    y_sample = xs[:, None, :]
    gla_s = _gla_update_sample(jnp.stack(ks), jnp.stack(decays), jnp.stack(vs), state_gla)
    conv_s = jnp.swapaxes(jnp.stack(glus), 1, 2)
    ffn_s = jnp.swapaxes(jnp.stack(ups), 1, 2)

    return (y_prompt, y_sample, jnp.stack(gla_p), gla_s, jnp.stack(conv_p), conv_s,
            jnp.stack(ffn_p), ffn_s)
```

```python
import functools

import jax
import jax.numpy as jnp
from jax import lax
from jax.experimental import pallas as pl
from jax.experimental.pallas import tpu as pltpu

D_MODEL = 1024
GLA_WIDTH = 512
CONV_CH = 512
GLA_HEADS = 4
GLA_DV = 128
GLA_KEY_WIDTH = 256
GLA_DK = 64
GATE_RANK = 16
GATE_TAU = 16.0
GLA_CHUNK = 64
CONV_WIDTH = 31
FFN_CONV_WIDTH = 3
D_FF = 2816
EPS = 1e-6

LANES = 128
SUBLANES = 8

CONV_BLOCKS = CONV_CH // LANES
C_U = 0
C_A = C_U + 2 * CONV_CH
C_Q = C_A + LANES
C_K = C_Q + GLA_KEY_WIDTH
C_V = C_K + GLA_KEY_WIDTH
C_G = C_V + GLA_WIDTH
IN_PAD = C_G + GLA_WIDTH

CONV_HIST = 32
FFN_HIST = SUBLANES
MIXER_TILE = 512
MIXER_PASS = 512
FFN_TILE = 256
CUMSUM_ROWS = 256
CONV_ROWS = 64
SAMPLE_GLA_BLOCK = 16
SAMPLE_GLA_UNROLL = 16

VMEM_LIMIT = 56 * 1024 * 1024


def _rms(x, w):
    return x * lax.rsqrt(jnp.mean(x * x, axis=-1, keepdims=True) + EPS) * w


def _log_sigmoid(z):
    return jnp.minimum(z, 0.0) - jnp.log1p(jnp.exp(-jnp.abs(z)))


def _silu(x):
    return x * jax.nn.sigmoid(x)


def _bf(x):
    return x.astype(jnp.bfloat16)


def _dot(a, b):
    return jnp.dot(a, b, preferred_element_type=jnp.float32)


def _dot_nt(a, b):
    return lax.dot_general(a, b, (((1,), (1,)), ((), ())), preferred_element_type=jnp.float32)


def _split3(x):
    hi = _bf(x)
    r = x - hi.astype(jnp.float32)
    mid = _bf(r)
    lo = _bf(r - mid.astype(jnp.float32))
    return hi, mid, lo


def _log_decay(alow, wgu_ref, bg_ref):
    z = _dot(_bf(alow), wgu_ref[...]) + bg_ref[...]
    return _log_sigmoid(z) * (1.0 / GATE_TAU)


def _const_spec(shape, layer):
    nd = len(shape)
    return pl.BlockSpec((None,) + tuple(shape), lambda *_: (layer,) + (0,) * nd,
                        pipeline_mode=pl.Buffered(1))


CONV_FIRST = CONV_HIST - (CONV_WIDTH - 1)


def _conv_block(cb, xp_sc, cv_sc, cw_ref, cb_ref, base, rows):
    cols = slice(cb * LANES, (cb + 1) * LANES)
    for r0 in range(base, base + rows, CONV_ROWS):
        acc = jnp.broadcast_to(cb_ref[:, cols], (CONV_ROWS, LANES))
        for j in range(CONV_WIDTH):
            s = r0 + CONV_FIRST + j
            acc = acc + cw_ref[j:j + 1, cols] * xp_sc[cb, s:s + CONV_ROWS, :]
        cv_sc[cb, r0:r0 + CONV_ROWS, :] = acc


def _gla_rows(aqk_sc, v_sc, g_sc, wgu_ref, bg_ref, gnw_ref, st_sc, oab_sc, base, rows):
    q0 = C_Q - C_A
    k0 = C_K - C_A
    tile = rows
    n_ch = tile // GLA_CHUNK
    span = min(tile, CUMSUM_ROWS)
    win = slice(base, base + rows)
    trow = lax.broadcasted_iota(jnp.int32, (span, span), 0)
    tcol = lax.broadcasted_iota(jnp.int32, (span, span), 1)
    same_block = jnp.bitwise_and(trow, -GLA_CHUNK) == jnp.bitwise_and(tcol, -GLA_CHUNK)
    tril = _bf(jnp.where(jnp.logical_and(same_block, tcol <= trow), 1.0, 0.0))
    srow = lax.broadcasted_iota(jnp.int32, (GLA_HEADS * GLA_CHUNK, GLA_CHUNK), 0)
    scol = lax.broadcasted_iota(jnp.int32, (GLA_HEADS * GLA_CHUNK, GLA_CHUNK), 1)
    causal = scol <= jnp.bitwise_and(srow, GLA_CHUNK - 1)
    lane = lax.broadcasted_iota(jnp.int32, (1, GLA_KEY_WIDTH), 1)
    head_masks = [jnp.logical_and(lane >= hh * GLA_DK, lane < (hh + 1) * GLA_DK)
                  for hh in range(GLA_HEADS)]
    blocks = [slice(c * GLA_CHUNK, (c + 1) * GLA_CHUNK) for c in range(n_ch)]

    def stack_heads(a):
        return _bf(jnp.concatenate([jnp.where(m, a, 0.0) for m in head_masks], axis=0))

    g = _log_decay(aqk_sc[win, 0:LANES], wgu_ref, bg_ref)
    ghi, gmid, glo = _split3(g)
    b = jnp.concatenate(
        [_dot(tril, ghi[r0:r0 + span]) + _dot(tril, gmid[r0:r0 + span]) + _dot(tril, glo[r0:r0 + span])
         for r0 in range(0, tile, span)], axis=0)
    last = [b[r.stop - 1:r.stop, :] for r in blocks]
    b_last = jnp.concatenate([jnp.broadcast_to(l, (GLA_CHUNK, GLA_KEY_WIDTH)) for l in last], axis=0)
    q = aqk_sc[win, q0:k0] * (GLA_DK ** -0.5)
    k = aqk_sc[win, k0:k0 + GLA_KEY_WIDTH]
    q_dec = q * jnp.exp(b)
    k_inv = _bf(k * jnp.exp(-b))
    k_rem = k * jnp.exp(b_last - b)

    q_stack = [stack_heads(q_dec[r]) for r in blocks]
    k_stack = [stack_heads(k_rem[r]) for r in blocks]
    scores = [jnp.where(causal, _dot_nt(q_stack[c], k_inv[blocks[c]]), 0.0) for c in range(n_ch)]
    tile_rows = [slice(base + r.start, base + r.stop) for r in blocks]
    v_blk = [v_sc[r, :] for r in tile_rows]
    v_heads = [[vb[:, hh * GLA_DV:(hh + 1) * GLA_DV] for hh in range(GLA_HEADS)] for vb in v_blk]
    kv = [_dot(_bf(jnp.concatenate(v_heads[c], axis=0).T), k_stack[c]) for c in range(n_ch)]
    o_intra = [jnp.concatenate(
        [_dot(_bf(scores[c][hh * GLA_CHUNK:(hh + 1) * GLA_CHUNK]), _bf(v_heads[c][hh]))
         for hh in range(GLA_HEADS)], axis=0) for c in range(n_ch)]

    st = st_sc[...]
    states = []
    for c in range(n_ch):
        states.append(_bf(st))
        st = jnp.exp(last[c]) * st + kv[c]
    st_sc[...] = st

    for c in range(n_ch):
        o = _dot_nt(q_stack[c], states[c]) + o_intra[c]
        o = o * lax.rsqrt(jnp.mean(o * o, axis=-1, keepdims=True) + EPS)
        for hh in range(GLA_HEADS):
            cols = slice(hh * GLA_DV, (hh + 1) * GLA_DV)
            oh = o[hh * GLA_CHUNK:(hh + 1) * GLA_CHUNK] * gnw_ref[:, cols]
            gate = g_sc[tile_rows[c], cols]
            oab_sc[tile_rows[c], cols] = _bf(oh * _silu(gate))


def _mixer_prompt_kernel(x_ref, n1_ref, win_ref, wgu_ref, bg_ref, gnw_ref, cw_ref, cb_ref,
                         lnw_ref, lnb_ref, wout_ref,
                         xo_ref, gla_ref, conv_ref,
                         aqk_sc, v_sc, g_sc, st_sc, xp_sc, oab_sc, h_sc, cv_sc, *, tile, n_tiles):
    i = pl.program_id(1)

    @pl.when(i == 0)
    def _():
        st_sc[...] = jnp.zeros_like(st_sc)
        xp_sc[:, 0:CONV_HIST, :] = jnp.zeros((CONV_BLOCKS, CONV_HIST, LANES), jnp.float32)

    h_sc[...] = _bf(_rms(x_ref[...], n1_ref[...]))
    rows = min(tile, MIXER_PASS)

    def project(c0, c1):
        return _dot(h_sc[...], win_ref[:, c0:c1])

    u = project(C_U, C_A)
    for cb in range(CONV_BLOCKS):
        c0 = cb * LANES
        xp_sc[cb, CONV_HIST:CONV_HIST + tile, :] = (
            u[:, c0:c0 + LANES] * jax.nn.sigmoid(u[:, CONV_CH + c0:CONV_CH + c0 + LANES]))
    aqk_sc[...] = project(C_A, C_V)
    v_sc[...] = project(C_V, C_G)
    g_sc[...] = project(C_G, IN_PAD)

    def conv(cb, base):
        _conv_block(cb, xp_sc, cv_sc, cw_ref, cb_ref, base, rows)

    def layer_norm_swish(base):
        win = slice(base, base + rows)
        cv = jnp.concatenate([cv_sc[cb, win, :] for cb in range(CONV_BLOCKS)], axis=1)
        mu = jnp.mean(cv, axis=-1, keepdims=True)
        var = jnp.mean(jnp.square(cv - mu), axis=-1, keepdims=True)
        y = (cv - mu) * lax.rsqrt(var + EPS) * lnw_ref[...] + lnb_ref[...]
        oab_sc[win, GLA_WIDTH:GLA_WIDTH + CONV_CH] = _bf(_silu(y))

    def finish(base):
        win = slice(base, base + rows)
        xo_ref[win, :] = x_ref[win, :] + _dot(oab_sc[win, :], wout_ref[...])

    for base in range(0, tile, rows):
        for cb in range(CONV_BLOCKS):
            conv(cb, base)
        layer_norm_swish(base)
        _gla_rows(aqk_sc, v_sc, g_sc, wgu_ref, bg_ref, gnw_ref, st_sc, oab_sc, base, rows)
        finish(base)
    xp_sc[:, 0:CONV_HIST, :] = xp_sc[:, tile:tile + CONV_HIST, :]

    @pl.when(i == n_tiles - 1)
    def _():
        gla_ref[...] = st_sc[...].T.reshape(GLA_HEADS, GLA_DK, GLA_DV)
        for cb in range(CONV_BLOCKS):
            conv_ref[:, cb * LANES:(cb + 1) * LANES] = xp_sc[cb, CONV_FIRST:CONV_HIST, :]


def _mixer_prompt(x, layer, p):
    bsz, t, _ = x.shape
    tile = MIXER_TILE
    n_tiles = t // tile
    kern = functools.partial(_mixer_prompt_kernel, tile=tile, n_tiles=n_tiles)
    return pl.pallas_call(
        kern,
        grid=(bsz, n_tiles),
        in_specs=[
            pl.BlockSpec((None, tile, D_MODEL), lambda b, i: (b, i, 0)),
            _const_spec((1, D_MODEL), layer),
            _const_spec((D_MODEL, IN_PAD), layer),
            _const_spec((LANES, GLA_KEY_WIDTH), layer),
            _const_spec((1, GLA_KEY_WIDTH), layer),
            _const_spec((1, GLA_WIDTH), layer),
            _const_spec((CONV_WIDTH, CONV_CH), layer),
            _const_spec((1, CONV_CH), layer),
            _const_spec((1, CONV_CH), layer),
            _const_spec((1, CONV_CH), layer),
            _const_spec((D_MODEL, D_MODEL), layer),
        ],
        out_specs=[
            pl.BlockSpec((None, tile, D_MODEL), lambda b, i: (b, i, 0)),
            pl.BlockSpec((None, GLA_HEADS, GLA_DK, GLA_DV), lambda b, i: (b, 0, 0, 0)),
            pl.BlockSpec((None, CONV_WIDTH - 1, CONV_CH), lambda b, i: (b, 0, 0)),
        ],
        out_shape=[
            jax.ShapeDtypeStruct((bsz, t, D_MODEL), jnp.float32),
            jax.ShapeDtypeStruct((bsz, GLA_HEADS, GLA_DK, GLA_DV), jnp.float32),
            jax.ShapeDtypeStruct((bsz, CONV_WIDTH - 1, CONV_CH), jnp.float32),
        ],
        scratch_shapes=[
            pltpu.VMEM((tile, C_V - C_A), jnp.float32),
            pltpu.VMEM((tile, GLA_WIDTH), jnp.float32),
            pltpu.VMEM((tile, GLA_WIDTH), jnp.float32),
            pltpu.VMEM((GLA_DV, GLA_KEY_WIDTH), jnp.float32),
            pltpu.VMEM((CONV_BLOCKS, CONV_HIST + tile, LANES), jnp.float32),
            pltpu.VMEM((tile, D_MODEL), jnp.bfloat16),
            pltpu.VMEM((tile, D_MODEL), jnp.bfloat16),
            pltpu.VMEM((CONV_BLOCKS, tile, LANES), jnp.float32),
        ],
        compiler_params=pltpu.CompilerParams(
            dimension_semantics=("arbitrary", "arbitrary"), vmem_limit_bytes=VMEM_LIMIT),
        name="mixer_prompt",
    )(x, p["norm1_w"], p["w_in"], p["w_gate_up"], p["b_gate"], p["gla_norm_w"], p["conv_w"],
      p["conv_b"], p["conv_ln_w"], p["conv_ln_b"], p["w_out"])


FF_BLOCKS = D_FF // LANES


FF_GROUP = 2


def _ffn_gate(up_r, cb, fw_ref, fb_ref, tile):
    def conv(blk):
        cols = slice(blk * LANES, (blk + 1) * LANES)
        acc = jnp.broadcast_to(fb_ref[:, cols], (tile, LANES))
        for j in range(FFN_CONV_WIDTH):
            r = FFN_HIST - (FFN_CONV_WIDTH - 1) + j
            acc = acc + fw_ref[j:j + 1, cols] * up_r[blk, r:r + tile, :]
        return acc

    return _bf(_silu(conv(cb)) * conv(FF_BLOCKS + cb))


def _ffn_prompt_kernel(x_ref, n2_ref, wup_ref, fw_ref, fb_ref, wdown_ref, nf_ref,
                       xo_ref, ffn_ref, up_sc, h_sc, gated_sc, *, tile, n_tiles, final_norm):
    i = pl.program_id(1)

    @pl.when(i == 0)
    def _():
        up_sc[:, 0:FFN_HIST, :] = jnp.zeros((2 * FF_BLOCKS, FFN_HIST, LANES), jnp.float32)

    x = x_ref[...]
    h_sc[...] = _bf(_rms(x, n2_ref[...]))
    xo_ref[...] = x
    width = FF_GROUP * LANES
    n_groups = FF_BLOCKS // FF_GROUP

    def up_group(grp):
        for half in range(2):
            c0 = half * D_FF + grp * width
            up = _dot(h_sc[...], wup_ref[:, c0:c0 + width])
            for j in range(FF_GROUP):
                up_sc[half * FF_BLOCKS + grp * FF_GROUP + j, FFN_HIST:FFN_HIST + tile, :] = (
                    up[:, j * LANES:(j + 1) * LANES])

    def gate_group(grp):
        for j in range(FF_GROUP):
            cb = grp * FF_GROUP + j
            gated_sc[:, cb * LANES:(cb + 1) * LANES] = _ffn_gate(up_sc, cb, fw_ref, fb_ref, tile)

    def down_group(grp):
        rows = slice(grp * width, (grp + 1) * width)
        xo_ref[...] += _dot(gated_sc[:, rows], wdown_ref[rows, :])

    for step in range(n_groups + 2):
        if step < n_groups:
            up_group(step)
        if 1 <= step <= n_groups:
            gate_group(step - 1)
        if step >= 2:
            down_group(step - 2)
    if final_norm:
        xo_ref[...] = _rms(xo_ref[...], nf_ref[...])

    @pl.when(i == n_tiles - 1)
    def _():
        for cb in range(2 * FF_BLOCKS):
            ffn_ref[:, cb * LANES:(cb + 1) * LANES] = up_sc[
                cb, FFN_HIST + tile - (FFN_CONV_WIDTH - 1):FFN_HIST + tile, :]

    up_sc[:, 0:FFN_HIST, :] = up_sc[:, tile:tile + FFN_HIST, :]


def _ffn_prompt(x, layer, p, final_norm):
    bsz, t, _ = x.shape
    tile = FFN_TILE
    n_tiles = t // tile
    kern = functools.partial(_ffn_prompt_kernel, tile=tile, n_tiles=n_tiles, final_norm=final_norm)
    return pl.pallas_call(
        kern,
        grid=(bsz, n_tiles),
        in_specs=[
            pl.BlockSpec((None, tile, D_MODEL), lambda b, i: (b, i, 0)),
            _const_spec((1, D_MODEL), layer),
            _const_spec((D_MODEL, 2 * D_FF), layer),
            _const_spec((FFN_CONV_WIDTH, 2 * D_FF), layer),
            _const_spec((1, 2 * D_FF), layer),
            _const_spec((D_FF, D_MODEL), layer),
            pl.BlockSpec((1, D_MODEL), lambda b, i: (0, 0), pipeline_mode=pl.Buffered(1)),
        ],
        out_specs=[
            pl.BlockSpec((None, tile, D_MODEL), lambda b, i: (b, i, 0)),
            pl.BlockSpec((None, FFN_CONV_WIDTH - 1, 2 * D_FF), lambda b, i: (b, 0, 0)),
        ],
        out_shape=[
            jax.ShapeDtypeStruct((bsz, t, D_MODEL), jnp.float32),
            jax.ShapeDtypeStruct((bsz, FFN_CONV_WIDTH - 1, 2 * D_FF), jnp.float32),
        ],
        scratch_shapes=[
            pltpu.VMEM((2 * FF_BLOCKS, FFN_HIST + tile, LANES), jnp.float32),
            pltpu.VMEM((tile, D_MODEL), jnp.bfloat16),
            pltpu.VMEM((tile, D_FF), jnp.bfloat16),
        ],
        compiler_params=pltpu.CompilerParams(
            dimension_semantics=("arbitrary", "arbitrary"), vmem_limit_bytes=VMEM_LIMIT),
        name="ffn_prompt",
    )(x, p["norm2_w"], p["w_up"], p["ffn_conv_w"], p["ffn_conv_b"], p["w_down"], p["norm_f_w"])


def _proj_sample_kernel(x_ref, n1_ref, win_ref, wgu_ref, bg_ref, cs_ref, cw_ref, cb_ref,
                        lnw_ref, lnb_ref,
                        qd_ref, k_ref, a_ref, v_ref, sc_ref, g_ref, glu_ref, ob_ref):
    x = x_ref[...]
    proj = _dot(_bf(_rms(x, n1_ref[...])), win_ref[...])
    g = _log_decay(proj[:, C_A:C_A + LANES], wgu_ref, bg_ref)
    q = proj[:, C_Q:C_K] * (GLA_DK ** -0.5)
    k = proj[:, C_K:C_V]
    q_dec = q * jnp.exp(g)
    k_inv = k * jnp.exp(-g)
    qd_ref[...] = q_dec
    k_ref[...] = k * jnp.exp(g - g)
    a_ref[...] = jnp.exp(g)
    v_ref[...] = proj[:, C_V:C_G]
    g_ref[...] = proj[:, C_G:IN_PAD]
    qk = q_dec * k_inv
    for hh in range(GLA_HEADS):
        sc_ref[:, hh * LANES:(hh + 1) * LANES] = jnp.broadcast_to(
            jnp.sum(qk[:, hh * GLA_DK:(hh + 1) * GLA_DK], axis=-1, keepdims=True),
            (x.shape[0], LANES))
    glu = proj[:, C_U:C_U + CONV_CH] * jax.nn.sigmoid(proj[:, C_U + CONV_CH:C_A])
    for j in range(CONV_WIDTH - 2):
        glu_ref[j] = cs_ref[j + 1]
    glu_ref[CONV_WIDTH - 2] = glu
    acc = cb_ref[...] + cw_ref[CONV_WIDTH - 1:CONV_WIDTH, :] * glu
    for j in range(CONV_WIDTH - 1):
        acc = acc + cw_ref[j:j + 1, :] * cs_ref[j]
    mu = jnp.mean(acc, axis=-1, keepdims=True)
    var = jnp.mean(jnp.square(acc - mu), axis=-1, keepdims=True)
    y = (acc - mu) * lax.rsqrt(var + EPS) * lnw_ref[...] + lnb_ref[...]
    ob_ref[...] = _silu(y)


def _proj_sample(x, cs_t, layer, p):
    n = x.shape[0]
    full = lambda shape: pl.BlockSpec(shape, lambda i: (0,) * len(shape))
    f32 = jnp.float32
    return pl.pallas_call(
        _proj_sample_kernel,
        grid=(1,),
        in_specs=[
            full((n, D_MODEL)),
            _const_spec((1, D_MODEL), layer),
            _const_spec((D_MODEL, IN_PAD), layer),
            _const_spec((LANES, GLA_KEY_WIDTH), layer),
            _const_spec((1, GLA_KEY_WIDTH), layer),
            pl.BlockSpec((None, CONV_WIDTH - 1, n, CONV_CH), lambda i: (layer, 0, 0, 0)),
            _const_spec((CONV_WIDTH, CONV_CH), layer),
            _const_spec((1, CONV_CH), layer),
            _const_spec((1, CONV_CH), layer),
            _const_spec((1, CONV_CH), layer),
        ],
        out_specs=[full((n, GLA_KEY_WIDTH)), full((n, GLA_KEY_WIDTH)), full((n, GLA_KEY_WIDTH)),
                   full((n, GLA_WIDTH)), full((n, GLA_WIDTH)), full((n, GLA_WIDTH)),
                   full((CONV_WIDTH - 1, n, CONV_CH)), full((n, CONV_CH))],
        out_shape=[jax.ShapeDtypeStruct((n, GLA_KEY_WIDTH), f32)] * 3
        + [jax.ShapeDtypeStruct((n, GLA_WIDTH), f32)] * 3
        + [jax.ShapeDtypeStruct((CONV_WIDTH - 1, n, CONV_CH), f32),
           jax.ShapeDtypeStruct((n, CONV_CH), f32)],
        compiler_params=pltpu.CompilerParams(
            dimension_semantics=("arbitrary",), vmem_limit_bytes=VMEM_LIMIT),
        name="proj_sample",
    )(x, p["norm1_w"], p["w_in"], p["w_gate_up"], p["b_gate"], cs_t, p["conv_w"], p["conv_b"],
      p["conv_ln_w"], p["conv_ln_b"])


GLA_ROWS = 2 * SUBLANES


def _head_row_mask():
    rid = lax.broadcasted_iota(jnp.int32, (GLA_ROWS, GLA_KEY_WIDTH), 0)
    lane = lax.broadcasted_iota(jnp.int32, (GLA_ROWS, GLA_KEY_WIDTH), 1)
    return jnp.logical_and(lane >= rid * GLA_DK, lane < (rid + 1) * GLA_DK)


def _gla_out_sample_kernel(qd_ref, s_ref, o_ref):
    head_row = _head_row_mask()

    def body(bb, carry):
        s = s_ref[bb].reshape(GLA_HEADS * GLA_DK, GLA_DV)
        q_rows = _bf(jnp.where(head_row, qd_ref[bb], 0.0))
        o_rows = _dot(q_rows, _bf(s))
        o_ref[bb] = jnp.concatenate([o_rows[hh:hh + 1] for hh in range(GLA_HEADS)], axis=1)
        return carry

    lax.fori_loop(0, SAMPLE_GLA_BLOCK, body, 0, unroll=SAMPLE_GLA_UNROLL)


def _gla_update_sample_kernel(k_ref, a_ref, v_ref, s_ref, so_ref):
    rows = GLA_ROWS
    head_row = _head_row_mask()
    vrid = lax.broadcasted_iota(jnp.int32, (rows, GLA_DV), 0)
    ones3 = _bf(jnp.where(vrid < 3, 1.0, 0.0))

    def body(bb, carry):
        s = s_ref[bb].reshape(GLA_HEADS * GLA_DK, GLA_DV)
        v = v_ref[bb]
        k_rows = _bf(jnp.where(head_row, k_ref[bb], 0.0))
        v_rows = _bf(jnp.concatenate(
            [v[:, hh * GLA_DV:(hh + 1) * GLA_DV] for hh in range(GLA_HEADS)]
            + [jnp.zeros((rows - GLA_HEADS, GLA_DV), jnp.float32)], axis=0))
        kv = lax.dot_general(k_rows, v_rows, (((0,), (0,)), ((), ())),
                             preferred_element_type=jnp.float32)
        a_hi, a_mid, a_lo = _split3(a_ref[bb])
        a_rows = jnp.concatenate(
            [a_hi, a_mid, a_lo, jnp.zeros((rows - 3, GLA_KEY_WIDTH), jnp.bfloat16)], axis=0)
        a_col = lax.dot_general(a_rows, ones3, (((0,), (0,)), ((), ())),
                                preferred_element_type=jnp.float32)
        so_ref[bb] = (a_col * s + kv).reshape(GLA_HEADS, GLA_DK, GLA_DV)
        return carry

    lax.fori_loop(0, SAMPLE_GLA_BLOCK, body, 0, unroll=SAMPLE_GLA_UNROLL)


def _gla_out_sample(qd, s_all, layer):
    n = qd.shape[0]
    blk = SAMPLE_GLA_BLOCK
    vec = lambda w: pl.BlockSpec((blk, 1, w), lambda i: (i, 0, 0))
    sspec = pl.BlockSpec((None, blk, GLA_HEADS, GLA_DK, GLA_DV), lambda i: (layer, i, 0, 0, 0))
    o = pl.pallas_call(
        _gla_out_sample_kernel,
        grid=(n // blk,),
        in_specs=[vec(GLA_KEY_WIDTH), sspec],
        out_specs=vec(GLA_WIDTH),
        out_shape=jax.ShapeDtypeStruct((n, 1, GLA_WIDTH), jnp.float32),
        compiler_params=pltpu.CompilerParams(
            dimension_semantics=("arbitrary",), vmem_limit_bytes=VMEM_LIMIT),
        name="gla_out_sample",
    )(qd[:, None, :], s_all)
    return o[:, 0, :]


def _gla_update_sample(k_all, a_all, v_all, s_all):
    depth, n = k_all.shape[0], k_all.shape[1]
    blk = SAMPLE_GLA_BLOCK
    vec = lambda w: pl.BlockSpec((None, blk, 1, w), lambda l, i: (l, i, 0, 0))
    sspec = pl.BlockSpec((None, blk, GLA_HEADS, GLA_DK, GLA_DV), lambda l, i: (l, i, 0, 0, 0))
    return pl.pallas_call(
        _gla_update_sample_kernel,
        grid=(depth, n // blk),
        in_specs=[vec(GLA_KEY_WIDTH), vec(GLA_KEY_WIDTH), vec(GLA_WIDTH), sspec],
        out_specs=sspec,
        out_shape=jax.ShapeDtypeStruct(s_all.shape, jnp.float32),
        compiler_params=pltpu.CompilerParams(
            dimension_semantics=("arbitrary", "arbitrary"), vmem_limit_bytes=VMEM_LIMIT),
        name="gla_update_sample",
    )(k_all[:, :, None, :], a_all[:, :, None, :], v_all[:, :, None, :], s_all)


def _tail_sample_kernel(x_ref, oi_ref, sc_ref, v_ref, g_ref, ob_ref, gnw_ref, wout_ref,
                        n2_ref, wup_ref, st0_ref, st1_ref, fw_ref, fb_ref, wdown_ref, nf_ref,
                        xo_ref, up_ref, *, final_norm):
    parts = []
    for hh in range(GLA_HEADS):
        cols = slice(hh * GLA_DV, (hh + 1) * GLA_DV)
        o = oi_ref[:, cols] + sc_ref[:, cols] * v_ref[:, cols]
        o = o * lax.rsqrt(jnp.mean(o * o, axis=-1, keepdims=True) + EPS)
        parts.append(_bf(o * gnw_ref[:, cols] * _silu(g_ref[:, cols])))
    parts.append(_bf(ob_ref[...]))
    x = x_ref[...] + _dot(jnp.concatenate(parts, axis=1), wout_ref[...])

    h2 = _bf(_rms(x, n2_ref[...]))
    up = _dot(h2, wup_ref[...])
    up_ref[0] = st1_ref[...]
    up_ref[1] = up
    upc = (fb_ref[...] + fw_ref[0:1, :] * st0_ref[...] + fw_ref[1:2, :] * st1_ref[...]
           + fw_ref[2:3, :] * up)
    gated = _bf(_silu(upc[:, :D_FF]) * upc[:, D_FF:])
    y = x + _dot(gated, wdown_ref[...])
    if final_norm:
        y = _rms(y, nf_ref[...])
    xo_ref[...] = y


def _tail_sample(x, oi, sc, v, g, ob, ffn_t, layer, p, final_norm):
    n = x.shape[0]
    full = lambda shape: pl.BlockSpec(shape, lambda i: (0,) * len(shape))
    plane = lambda j: pl.BlockSpec((None, None, n, 2 * D_FF), lambda i: (layer, j, 0, 0))
    return pl.pallas_call(
        functools.partial(_tail_sample_kernel, final_norm=final_norm),
        grid=(1,),
        in_specs=[
            full((n, D_MODEL)), full((n, GLA_WIDTH)), full((n, GLA_WIDTH)), full((n, GLA_WIDTH)),
            full((n, GLA_WIDTH)), full((n, CONV_CH)),
            _const_spec((1, GLA_WIDTH), layer),
            _const_spec((D_MODEL, D_MODEL), layer),
            _const_spec((1, D_MODEL), layer),
            _const_spec((D_MODEL, 2 * D_FF), layer),
            plane(0), plane(1),
            _const_spec((FFN_CONV_WIDTH, 2 * D_FF), layer),
            _const_spec((1, 2 * D_FF), layer),
            _const_spec((D_FF, D_MODEL), layer),
            pl.BlockSpec((1, D_MODEL), lambda i: (0, 0)),
        ],
        out_specs=[full((n, D_MODEL)), full((FFN_CONV_WIDTH - 1, n, 2 * D_FF))],
        out_shape=[jax.ShapeDtypeStruct((n, D_MODEL), jnp.float32),
                   jax.ShapeDtypeStruct((FFN_CONV_WIDTH - 1, n, 2 * D_FF), jnp.float32)],
        compiler_params=pltpu.CompilerParams(
            dimension_semantics=("arbitrary",), vmem_limit_bytes=VMEM_LIMIT),
        name="tail_sample",
    )(x, oi, sc, v, g, ob, p["gla_norm_w"], p["w_out"], p["norm2_w"], p["w_up"], ffn_t, ffn_t,
      p["ffn_conv_w"], p["ffn_conv_b"], p["w_down"], p["norm_f_w"])


W_IN_ROWS = 256


def _w_in_layout_kernel(w_ref, o_ref):
    a0 = 2 * GLA_KEY_WIDTH + 2 * GLA_WIDTH
    tail = w_ref[:, a0:]
    o_ref[:, C_U:C_A] = _bf(tail[:, GATE_RANK:])
    lane = lax.broadcasted_iota(jnp.int32, (W_IN_ROWS, LANES), 1)
    o_ref[:, C_A:C_Q] = _bf(jnp.where(lane < GATE_RANK, tail[:, 0:LANES], 0.0))
    o_ref[:, C_Q:IN_PAD] = _bf(w_ref[:, 0:a0])


def _w_in_layout(w_in):
    depth, d, cols = w_in.shape
    return pl.pallas_call(
        _w_in_layout_kernel,
        grid=(depth, d // W_IN_ROWS),
        in_specs=[pl.BlockSpec((None, W_IN_ROWS, cols), lambda l, r: (l, r, 0))],
        out_specs=pl.BlockSpec((None, W_IN_ROWS, IN_PAD), lambda l, r: (l, r, 0)),
        out_shape=jax.ShapeDtypeStruct((depth, d, IN_PAD), jnp.bfloat16),
        compiler_params=pltpu.CompilerParams(dimension_semantics=("arbitrary", "arbitrary")),
        name="w_in_layout",
    )(w_in)


def _prepare_params(norm1_w, w_in, w_gate_up, b_gate, gla_norm_w, conv_w, conv_b, conv_ln_w,
                    conv_ln_b, w_out, norm2_w, w_up, ffn_conv_w, ffn_conv_b, w_down, norm_f_w):
    depth = w_in.shape[0]
    wgu = jnp.concatenate(
        [w_gate_up, jnp.zeros((depth, LANES - GATE_RANK, GLA_KEY_WIDTH), w_gate_up.dtype)], axis=1)
    row = lambda a: a[:, None, :]
    return {
        "norm1_w": row(norm1_w), "w_in": _w_in_layout(w_in), "w_gate_up": _bf(wgu),
        "b_gate": row(b_gate),
        "gla_norm_w": row(gla_norm_w), "conv_w": conv_w, "conv_b": row(conv_b),
        "conv_ln_w": row(conv_ln_w), "conv_ln_b": row(conv_ln_b), "w_out": _bf(w_out),
        "norm2_w": row(norm2_w), "w_up": _bf(w_up), "ffn_conv_w": ffn_conv_w,
        "ffn_conv_b": row(ffn_conv_b), "w_down": _bf(w_down), "norm_f_w": norm_f_w[None, :],
    }


def kernel(x_prompt, x_sample, state_gla, state_conv, state_ffn, norm1_w, w_in, w_gate_up, b_gate,
           gla_norm_w, conv_w, conv_b, conv_ln_w, conv_ln_b, w_out, norm2_w, w_up, ffn_conv_w,
           ffn_conv_b, w_down, norm_f_w):
    p = _prepare_params(norm1_w, w_in, w_gate_up, b_gate, gla_norm_w, conv_w, conv_b, conv_ln_w,
                        conv_ln_b, w_out, norm2_w, w_up, ffn_conv_w, ffn_conv_b, w_down, norm_f_w)
    depth = w_in.shape[0]

    x = x_prompt
    gla_p, conv_p, ffn_p = [], [], []
    for l in range(depth):
        x, s_new, c_buf = _mixer_prompt(x, l, p)
        x, f_buf = _ffn_prompt(x, l, p, final_norm=(l == depth - 1))
        gla_p.append(s_new)
        conv_p.append(c_buf)
        ffn_p.append(f_buf)
    y_prompt = x

    xs = x_sample[:, 0, :]
    conv_t = jnp.swapaxes(state_conv, 1, 2)
    ffn_t = jnp.swapaxes(state_ffn, 1, 2)
    glus, ups, ks, decays, vs = [], [], [], [], []
    for l in range(depth):
        qd, k, a, v, sc, g, glu, ob = _proj_sample(xs, conv_t, l, p)
        oi = _gla_out_sample(qd, state_gla, l)
        xs, up = _tail_sample(xs, oi, sc, v, g, ob, ffn_t, l, p, final_norm=(l == depth - 1))
        glus.append(glu)
        ups.append(up)
        ks.append(k)
        decays.append(a)
        vs.append(v)
    y_sample = xs[:, None, :]
    gla_s = _gla_update_sample(jnp.stack(ks), jnp.stack(decays), jnp.stack(vs), state_gla)
    conv_s = jnp.swapaxes(jnp.stack(glus), 1, 2)
    ffn_s = jnp.swapaxes(jnp.stack(ups), 1, 2)

    return (y_prompt, y_sample, jnp.stack(gla_p), gla_s, jnp.stack(conv_p), conv_s,
            jnp.stack(ffn_p), ffn_s)
```

```python
import functools

import jax
import jax.numpy as jnp
from jax import lax
from jax.experimental import pallas as pl
from jax.experimental.pallas import tpu as pltpu

D_MODEL = 1024
GLA_WIDTH = 512
CONV_CH = 512
GLA_HEADS = 4
GLA_DV = 128
GLA_KEY_WIDTH = 256
GLA_DK = 64
GATE_RANK = 16
GATE_TAU = 16.0
GLA_CHUNK = 64
CONV_WIDTH = 31
FFN_CONV_WIDTH = 3
D_FF = 2816
EPS = 1e-6

LANES = 128
SUBLANES = 8

CONV_BLOCKS = CONV_CH // LANES
C_U = 0
C_A = C_U + 2 * CONV_CH
C_Q = C_A + LANES
C_K = C_Q + GLA_KEY_WIDTH
C_V = C_K + GLA_KEY_WIDTH
C_G = C_V + GLA_WIDTH
IN_PAD = C_G + GLA_WIDTH

CONV_HIST = 32
FFN_HIST = SUBLANES
MIXER_TILE = 512
MIXER_PASS = 512
FFN_TILE = 256
CUMSUM_ROWS = 256
CONV_ROWS = 64
SAMPLE_GLA_BLOCK = 16
SAMPLE_GLA_UNROLL = 16

VMEM_LIMIT = 56 * 1024 * 1024


def _rms(x, w):
    return x * lax.rsqrt(jnp.mean(x * x, axis=-1, keepdims=True) + EPS) * w


def _log_sigmoid(z):
    return jnp.minimum(z, 0.0) - jnp.log1p(jnp.exp(-jnp.abs(z)))


def _silu(x):
    return x * jax.nn.sigmoid(x)


def _bf(x):
    return x.astype(jnp.bfloat16)


def _dot(a, b):
    return jnp.dot(a, b, preferred_element_type=jnp.float32)


def _dot_nt(a, b):
    return lax.dot_general(a, b, (((1,), (1,)), ((), ())), preferred_element_type=jnp.float32)


def _split3(x):
    hi = _bf(x)
    r = x - hi.astype(jnp.float32)
    mid = _bf(r)
    lo = _bf(r - mid.astype(jnp.float32))
    return hi, mid, lo


def _log_decay(alow, wgu_ref, bg_ref):
    z = _dot(_bf(alow), wgu_ref[...]) + bg_ref[...]
    return _log_sigmoid(z) * (1.0 / GATE_TAU)


def _const_spec(shape, layer):
    nd = len(shape)
    return pl.BlockSpec((None,) + tuple(shape), lambda *_: (layer,) + (0,) * nd,
                        pipeline_mode=pl.Buffered(1))


CONV_FIRST = CONV_HIST - (CONV_WIDTH - 1)


def _conv_block(cb, xp_sc, cv_sc, cw_ref, cb_ref, base, rows):
    cols = slice(cb * LANES, (cb + 1) * LANES)
    for r0 in range(base, base + rows, CONV_ROWS):
        acc = jnp.broadcast_to(cb_ref[:, cols], (CONV_ROWS, LANES))
        for j in range(CONV_WIDTH):
            s = r0 + CONV_FIRST + j
            acc = acc + cw_ref[j:j + 1, cols] * xp_sc[cb, s:s + CONV_ROWS, :]
        cv_sc[cb, r0:r0 + CONV_ROWS, :] = acc


def _gla_rows(aqk_sc, v_sc, g_sc, wgu_ref, bg_ref, gnw_ref, st_sc, oab_sc, base, rows):
    q0 = C_Q - C_A
    k0 = C_K - C_A
    tile = rows
    n_ch = tile // GLA_CHUNK
    span = min(tile, CUMSUM_ROWS)
    win = slice(base, base + rows)
    trow = lax.broadcasted_iota(jnp.int32, (span, span), 0)
    tcol = lax.broadcasted_iota(jnp.int32, (span, span), 1)
    same_block = jnp.bitwise_and(trow, -GLA_CHUNK) == jnp.bitwise_and(tcol, -GLA_CHUNK)
    tril = _bf(jnp.where(jnp.logical_and(same_block, tcol <= trow), 1.0, 0.0))
    srow = lax.broadcasted_iota(jnp.int32, (GLA_HEADS * GLA_CHUNK, GLA_CHUNK), 0)
    scol = lax.broadcasted_iota(jnp.int32, (GLA_HEADS * GLA_CHUNK, GLA_CHUNK), 1)
    causal = scol <= jnp.bitwise_and(srow, GLA_CHUNK - 1)
    lane = lax.broadcasted_iota(jnp.int32, (1, GLA_KEY_WIDTH), 1)
    head_masks = [jnp.logical_and(lane >= hh * GLA_DK, lane < (hh + 1) * GLA_DK)
                  for hh in range(GLA_HEADS)]
    blocks = [slice(c * GLA_CHUNK, (c + 1) * GLA_CHUNK) for c in range(n_ch)]

    def stack_heads(a):
        return _bf(jnp.concatenate([jnp.where(m, a, 0.0) for m in head_masks], axis=0))

    g = _log_decay(aqk_sc[win, 0:LANES], wgu_ref, bg_ref)
    ghi, gmid, glo = _split3(g)
    b = jnp.concatenate(
        [_dot(tril, ghi[r0:r0 + span]) + _dot(tril, gmid[r0:r0 + span]) + _dot(tril, glo[r0:r0 + span])
         for r0 in range(0, tile, span)], axis=0)
    last = [b[r.stop - 1:r.stop, :] for r in blocks]
    b_last = jnp.concatenate([jnp.broadcast_to(l, (GLA_CHUNK, GLA_KEY_WIDTH)) for l in last], axis=0)
    q = aqk_sc[win, q0:k0] * (GLA_DK ** -0.5)
    k = aqk_sc[win, k0:k0 + GLA_KEY_WIDTH]
    q_dec = q * jnp.exp(b)
    k_inv = _bf(k * jnp.exp(-b))
    k_rem = k * jnp.exp(b_last - b)

    q_stack = [stack_heads(q_dec[r]) for r in blocks]
    k_stack = [stack_heads(k_rem[r]) for r in blocks]
    scores = [jnp.where(causal, _dot_nt(q_stack[c], k_inv[blocks[c]]), 0.0) for c in range(n_ch)]
    tile_rows = [slice(base + r.start, base + r.stop) for r in blocks]
    v_blk = [v_sc[r, :] for r in tile_rows]
    v_heads = [[vb[:, hh * GLA_DV:(hh + 1) * GLA_DV] for hh in range(GLA_HEADS)] for vb in v_blk]
    kv = [_dot(_bf(jnp.concatenate(v_heads[c], axis=0).T), k_stack[c]) for c in range(n_ch)]
    o_intra = [jnp.concatenate(
        [_dot(_bf(scores[c][hh * GLA_CHUNK:(hh + 1) * GLA_CHUNK]), _bf(v_heads[c][hh]))
         for hh in range(GLA_HEADS)], axis=0) for c in range(n_ch)]

    st = st_sc[...]
    states = []
    for c in range(n_ch):
        states.append(_bf(st))
        st = jnp.exp(last[c]) * st + kv[c]
    st_sc[...] = st

    for c in range(n_ch):
        o = _dot_nt(q_stack[c], states[c]) + o_intra[c]
        o = o * lax.rsqrt(jnp.mean(o * o, axis=-1, keepdims=True) + EPS)
        for hh in range(GLA_HEADS):
            cols = slice(hh * GLA_DV, (hh + 1) * GLA_DV)
            oh = o[hh * GLA_CHUNK:(hh + 1) * GLA_CHUNK] * gnw_ref[:, cols]
            gate = g_sc[tile_rows[c], cols]
            oab_sc[tile_rows[c], cols] = _bf(oh * _silu(gate))


def _mixer_prompt_kernel(x_ref, n1_ref, win_ref, wgu_ref, bg_ref, gnw_ref, cw_ref, cb_ref,
                         lnw_ref, lnb_ref, wout_ref,
                         xo_ref, gla_ref, conv_ref,
                         aqk_sc, v_sc, g_sc, st_sc, xp_sc, oab_sc, h_sc, cv_sc, *, tile, n_tiles):
    i = pl.program_id(1)

    @pl.when(i == 0)
    def _():
        st_sc[...] = jnp.zeros_like(st_sc)
        xp_sc[:, 0:CONV_HIST, :] = jnp.zeros((CONV_BLOCKS, CONV_HIST, LANES), jnp.float32)

    h_sc[...] = _bf(_rms(x_ref[...], n1_ref[...]))
    rows = min(tile, MIXER_PASS)

    def project(c0, c1):
        return _dot(h_sc[...], win_ref[:, c0:c1])

    u = project(C_U, C_A)
    for cb in range(CONV_BLOCKS):
        c0 = cb * LANES
        xp_sc[cb, CONV_HIST:CONV_HIST + tile, :] = (
            u[:, c0:c0 + LANES] * jax.nn.sigmoid(u[:, CONV_CH + c0:CONV_CH + c0 + LANES]))
    aqk_sc[...] = project(C_A, C_V)
    v_sc[...] = project(C_V, C_G)
    g_sc[...] = project(C_G, IN_PAD)

    def conv(cb, base):
        _conv_block(cb, xp_sc, cv_sc, cw_ref, cb_ref, base, rows)

    def layer_norm_swish(base):
        win = slice(base, base + rows)
        cv = jnp.concatenate([cv_sc[cb, win, :] for cb in range(CONV_BLOCKS)], axis=1)
        mu = jnp.mean(cv, axis=-1, keepdims=True)
        var = jnp.mean(jnp.square(cv - mu), axis=-1, keepdims=True)
        y = (cv - mu) * lax.rsqrt(var + EPS) * lnw_ref[...] + lnb_ref[...]
        oab_sc[win, GLA_WIDTH:GLA_WIDTH + CONV_CH] = _bf(_silu(y))

    def finish(base):
        win = slice(base, base + rows)
        xo_ref[win, :] = x_ref[win, :] + _dot(oab_sc[win, :], wout_ref[...])

    for base in range(0, tile, rows):
        for cb in range(CONV_BLOCKS):
            conv(cb, base)
        layer_norm_swish(base)
        _gla_rows(aqk_sc, v_sc, g_sc, wgu_ref, bg_ref, gnw_ref, st_sc, oab_sc, base, rows)
        finish(base)
    xp_sc[:, 0:CONV_HIST, :] = xp_sc[:, tile:tile + CONV_HIST, :]

    @pl.when(i == n_tiles - 1)
    def _():
        gla_ref[...] = st_sc[...].T.reshape(GLA_HEADS, GLA_DK, GLA_DV)
        for cb in range(CONV_BLOCKS):
            conv_ref[:, cb * LANES:(cb + 1) * LANES] = xp_sc[cb, CONV_FIRST:CONV_HIST, :]


def _mixer_prompt(x, layer, p):
    bsz, t, _ = x.shape
    tile = MIXER_TILE
    n_tiles = t // tile
    kern = functools.partial(_mixer_prompt_kernel, tile=tile, n_tiles=n_tiles)
    return pl.pallas_call(
        kern,
        grid=(bsz, n_tiles),
        in_specs=[
            pl.BlockSpec((None, tile, D_MODEL), lambda b, i: (b, i, 0)),
            _const_spec((1, D_MODEL), layer),
            _const_spec((D_MODEL, IN_PAD), layer),
            _const_spec((LANES, GLA_KEY_WIDTH), layer),
            _const_spec((1, GLA_KEY_WIDTH), layer),
            _const_spec((1, GLA_WIDTH), layer),
            _const_spec((CONV_WIDTH, CONV_CH), layer),
            _const_spec((1, CONV_CH), layer),
            _const_spec((1, CONV_CH), layer),
            _const_spec((1, CONV_CH), layer),
            _const_spec((D_MODEL, D_MODEL), layer),
        ],
        out_specs=[
            pl.BlockSpec((None, tile, D_MODEL), lambda b, i: (b, i, 0)),
            pl.BlockSpec((None, GLA_HEADS, GLA_DK, GLA_DV), lambda b, i: (b, 0, 0, 0)),
            pl.BlockSpec((None, CONV_WIDTH - 1, CONV_CH), lambda b, i: (b, 0, 0)),
        ],
        out_shape=[
            jax.ShapeDtypeStruct((bsz, t, D_MODEL), jnp.float32),
            jax.ShapeDtypeStruct((bsz, GLA_HEADS, GLA_DK, GLA_DV), jnp.float32),
            jax.ShapeDtypeStruct((bsz, CONV_WIDTH - 1, CONV_CH), jnp.float32),
        ],
        scratch_shapes=[
            pltpu.VMEM((tile, C_V - C_A), jnp.float32),
            pltpu.VMEM((tile, GLA_WIDTH), jnp.float32),
            pltpu.VMEM((tile, GLA_WIDTH), jnp.float32),
            pltpu.VMEM((GLA_DV, GLA_KEY_WIDTH), jnp.float32),
            pltpu.VMEM((CONV_BLOCKS, CONV_HIST + tile, LANES), jnp.float32),
            pltpu.VMEM((tile, D_MODEL), jnp.bfloat16),
            pltpu.VMEM((tile, D_MODEL), jnp.bfloat16),
            pltpu.VMEM((CONV_BLOCKS, tile, LANES), jnp.float32),
        ],
        compiler_params=pltpu.CompilerParams(
            dimension_semantics=("arbitrary", "arbitrary"), vmem_limit_bytes=VMEM_LIMIT),
        name="mixer_prompt",
    )(x, p["norm1_w"], p["w_in"], p["w_gate_up"], p["b_gate"], p["gla_norm_w"], p["conv_w"],
      p["conv_b"], p["conv_ln_w"], p["conv_ln_b"], p["w_out"])


FF_BLOCKS = D_FF // LANES


FF_GROUP = 11


def _ffn_gate(up_r, cb, fw_ref, fb_ref, tile):
    def conv(blk):
        cols = slice(blk * LANES, (blk + 1) * LANES)
        acc = jnp.broadcast_to(fb_ref[:, cols], (tile, LANES))
        for j in range(FFN_CONV_WIDTH):
            r = FFN_HIST - (FFN_CONV_WIDTH - 1) + j
            acc = acc + fw_ref[j:j + 1, cols] * up_r[blk, r:r + tile, :]
        return acc

    return _bf(_silu(conv(cb)) * conv(FF_BLOCKS + cb))


def _ffn_prompt_kernel(x_ref, n2_ref, wup_ref, fw_ref, fb_ref, wdown_ref, nf_ref,
                       xo_ref, ffn_ref, up_sc, h_sc, gated_sc, *, tile, n_tiles, final_norm):
    i = pl.program_id(1)

    @pl.when(i == 0)
    def _():
        up_sc[:, 0:FFN_HIST, :] = jnp.zeros((2 * FF_BLOCKS, FFN_HIST, LANES), jnp.float32)

    x = x_ref[...]
    h_sc[...] = _bf(_rms(x, n2_ref[...]))
    xo_ref[...] = x
    width = FF_GROUP * LANES
    n_groups = FF_BLOCKS // FF_GROUP

    def up_group(grp):
        for half in range(2):
            c0 = half * D_FF + grp * width
            up = _dot(h_sc[...], wup_ref[:, c0:c0 + width])
            for j in range(FF_GROUP):
                up_sc[half * FF_BLOCKS + grp * FF_GROUP + j, FFN_HIST:FFN_HIST + tile, :] = (
                    up[:, j * LANES:(j + 1) * LANES])

    def gate_group(grp):
        for j in range(FF_GROUP):
            cb = grp * FF_GROUP + j
            gated_sc[:, cb * LANES:(cb + 1) * LANES] = _ffn_gate(up_sc, cb, fw_ref, fb_ref, tile)

    def down_group(grp):
        rows = slice(grp * width, (grp + 1) * width)
        xo_ref[...] += _dot(gated_sc[:, rows], wdown_ref[rows, :])

    for step in range(n_groups + 2):
        if step < n_groups:
            up_group(step)
        if 1 <= step <= n_groups:
            gate_group(step - 1)
        if step >= 2:
            down_group(step - 2)
    if final_norm:
        xo_ref[...] = _rms(xo_ref[...], nf_ref[...])

    @pl.when(i == n_tiles - 1)
    def _():
        for cb in range(2 * FF_BLOCKS):
            ffn_ref[:, cb * LANES:(cb + 1) * LANES] = up_sc[
                cb, FFN_HIST + tile - (FFN_CONV_WIDTH - 1):FFN_HIST + tile, :]

    up_sc[:, 0:FFN_HIST, :] = up_sc[:, tile:tile + FFN_HIST, :]


def _ffn_prompt(x, layer, p, final_norm):
    bsz, t, _ = x.shape
    tile = FFN_TILE
    n_tiles = t // tile
    kern = functools.partial(_ffn_prompt_kernel, tile=tile, n_tiles=n_tiles, final_norm=final_norm)
    return pl.pallas_call(
        kern,
        grid=(bsz, n_tiles),
        in_specs=[
            pl.BlockSpec((None, tile, D_MODEL), lambda b, i: (b, i, 0)),
            _const_spec((1, D_MODEL), layer),
            _const_spec((D_MODEL, 2 * D_FF), layer),
            _const_spec((FFN_CONV_WIDTH, 2 * D_FF), layer),
            _const_spec((1, 2 * D_FF), layer),
            _const_spec((D_FF, D_MODEL), layer),
            pl.BlockSpec((1, D_MODEL), lambda b, i: (0, 0), pipeline_mode=pl.Buffered(1)),
        ],
        out_specs=[
            pl.BlockSpec((None, tile, D_MODEL), lambda b, i: (b, i, 0)),
            pl.BlockSpec((None, FFN_CONV_WIDTH - 1, 2 * D_FF), lambda b, i: (b, 0, 0)),
        ],
        out_shape=[
            jax.ShapeDtypeStruct((bsz, t, D_MODEL), jnp.float32),
            jax.ShapeDtypeStruct((bsz, FFN_CONV_WIDTH - 1, 2 * D_FF), jnp.float32),
        ],
        scratch_shapes=[
            pltpu.VMEM((2 * FF_BLOCKS, FFN_HIST + tile, LANES), jnp.float32),
            pltpu.VMEM((tile, D_MODEL), jnp.bfloat16),
            pltpu.VMEM((tile, D_FF), jnp.bfloat16),
        ],
        compiler_params=pltpu.CompilerParams(
            dimension_semantics=("arbitrary", "arbitrary"), vmem_limit_bytes=VMEM_LIMIT),
        name="ffn_prompt",
    )(x, p["norm2_w"], p["w_up"], p["ffn_conv_w"], p["ffn_conv_b"], p["w_down"], p["norm_f_w"])


def _proj_sample_kernel(x_ref, n1_ref, win_ref, wgu_ref, bg_ref, cs_ref, cw_ref, cb_ref,
                        lnw_ref, lnb_ref,
                        qd_ref, k_ref, a_ref, v_ref, sc_ref, g_ref, glu_ref, ob_ref):
    x = x_ref[...]
    proj = _dot(_bf(_rms(x, n1_ref[...])), win_ref[...])
    g = _log_decay(proj[:, C_A:C_A + LANES], wgu_ref, bg_ref)
    q = proj[:, C_Q:C_K] * (GLA_DK ** -0.5)
    k = proj[:, C_K:C_V]
    q_dec = q * jnp.exp(g)
    k_inv = k * jnp.exp(-g)
    qd_ref[...] = q_dec
    k_ref[...] = k * jnp.exp(g - g)
    a_ref[...] = jnp.exp(g)
    v_ref[...] = proj[:, C_V:C_G]
    g_ref[...] = proj[:, C_G:IN_PAD]
    qk = q_dec * k_inv
    for hh in range(GLA_HEADS):
        sc_ref[:, hh * LANES:(hh + 1) * LANES] = jnp.broadcast_to(
            jnp.sum(qk[:, hh * GLA_DK:(hh + 1) * GLA_DK], axis=-1, keepdims=True),
            (x.shape[0], LANES))
    glu = proj[:, C_U:C_U + CONV_CH] * jax.nn.sigmoid(proj[:, C_U + CONV_CH:C_A])
    for j in range(CONV_WIDTH - 2):
        glu_ref[j] = cs_ref[j + 1]
    glu_ref[CONV_WIDTH - 2] = glu
    acc = cb_ref[...] + cw_ref[CONV_WIDTH - 1:CONV_WIDTH, :] * glu
    for j in range(CONV_WIDTH - 1):
        acc = acc + cw_ref[j:j + 1, :] * cs_ref[j]
    mu = jnp.mean(acc, axis=-1, keepdims=True)
    var = jnp.mean(jnp.square(acc - mu), axis=-1, keepdims=True)
    y = (acc - mu) * lax.rsqrt(var + EPS) * lnw_ref[...] + lnb_ref[...]
    ob_ref[...] = _silu(y)


def _proj_sample(x, cs_t, layer, p):
    n = x.shape[0]
    full = lambda shape: pl.BlockSpec(shape, lambda i: (0,) * len(shape))
    f32 = jnp.float32
    return pl.pallas_call(
        _proj_sample_kernel,
        grid=(1,),
        in_specs=[
            full((n, D_MODEL)),
            _const_spec((1, D_MODEL), layer),
            _const_spec((D_MODEL, IN_PAD), layer),
            _const_spec((LANES, GLA_KEY_WIDTH), layer),
            _const_spec((1, GLA_KEY_WIDTH), layer),
            pl.BlockSpec((None, CONV_WIDTH - 1, n, CONV_CH), lambda i: (layer, 0, 0, 0)),
            _const_spec((CONV_WIDTH, CONV_CH), layer),
            _const_spec((1, CONV_CH), layer),
            _const_spec((1, CONV_CH), layer),
            _const_spec((1, CONV_CH), layer),
        ],
        out_specs=[full((n, GLA_KEY_WIDTH)), full((n, GLA_KEY_WIDTH)), full((n, GLA_KEY_WIDTH)),
                   full((n, GLA_WIDTH)), full((n, GLA_WIDTH)), full((n, GLA_WIDTH)),
                   full((CONV_WIDTH - 1, n, CONV_CH)), full((n, CONV_CH))],
        out_shape=[jax.ShapeDtypeStruct((n, GLA_KEY_WIDTH), f32)] * 3
        + [jax.ShapeDtypeStruct((n, GLA_WIDTH), f32)] * 3
        + [jax.ShapeDtypeStruct((CONV_WIDTH - 1, n, CONV_CH), f32),
           jax.ShapeDtypeStruct((n, CONV_CH), f32)],
        compiler_params=pltpu.CompilerParams(
            dimension_semantics=("arbitrary",), vmem_limit_bytes=VMEM_LIMIT),
        name="proj_sample",
    )(x, p["norm1_w"], p["w_in"], p["w_gate_up"], p["b_gate"], cs_t, p["conv_w"], p["conv_b"],
      p["conv_ln_w"], p["conv_ln_b"])


GLA_ROWS = 2 * SUBLANES


def _head_row_mask():
    rid = lax.broadcasted_iota(jnp.int32, (GLA_ROWS, GLA_KEY_WIDTH), 0)
    lane = lax.broadcasted_iota(jnp.int32, (GLA_ROWS, GLA_KEY_WIDTH), 1)
    return jnp.logical_and(lane >= rid * GLA_DK, lane < (rid + 1) * GLA_DK)


def _gla_out_sample_kernel(qd_ref, s_ref, o_ref):
    head_row = _head_row_mask()

    def body(bb, carry):
        s = s_ref[bb].reshape(GLA_HEADS * GLA_DK, GLA_DV)
        q_rows = _bf(jnp.where(head_row, qd_ref[bb], 0.0))
        o_rows = _dot(q_rows, _bf(s))
        o_ref[bb] = jnp.concatenate([o_rows[hh:hh + 1] for hh in range(GLA_HEADS)], axis=1)
        return carry

    lax.fori_loop(0, SAMPLE_GLA_BLOCK, body, 0, unroll=SAMPLE_GLA_UNROLL)


def _gla_update_sample_kernel(k_ref, a_ref, v_ref, s_ref, so_ref):
    rows = GLA_ROWS
    head_row = _head_row_mask()
    vrid = lax.broadcasted_iota(jnp.int32, (rows, GLA_DV), 0)
    ones3 = _bf(jnp.where(vrid < 3, 1.0, 0.0))

    def body(bb, carry):
        s = s_ref[bb].reshape(GLA_HEADS * GLA_DK, GLA_DV)
        v = v_ref[bb]
        k_rows = _bf(jnp.where(head_row, k_ref[bb], 0.0))
        v_rows = _bf(jnp.concatenate(
            [v[:, hh * GLA_DV:(hh + 1) * GLA_DV] for hh in range(GLA_HEADS)]
            + [jnp.zeros((rows - GLA_HEADS, GLA_DV), jnp.float32)], axis=0))
        kv = lax.dot_general(k_rows, v_rows, (((0,), (0,)), ((), ())),
                             preferred_element_type=jnp.float32)
        a_hi, a_mid, a_lo = _split3(a_ref[bb])
        a_rows = jnp.concatenate(
            [a_hi, a_mid, a_lo, jnp.zeros((rows - 3, GLA_KEY_WIDTH), jnp.bfloat16)], axis=0)
        a_col = lax.dot_general(a_rows, ones3, (((0,), (0,)), ((), ())),
                                preferred_element_type=jnp.float32)
        so_ref[bb] = (a_col * s + kv).reshape(GLA_HEADS, GLA_DK, GLA_DV)
        return carry

    lax.fori_loop(0, SAMPLE_GLA_BLOCK, body, 0, unroll=SAMPLE_GLA_UNROLL)


def _gla_out_sample(qd, s_all, layer):
    n = qd.shape[0]
    blk = SAMPLE_GLA_BLOCK
    vec = lambda w: pl.BlockSpec((blk, 1, w), lambda i: (i, 0, 0))
    sspec = pl.BlockSpec((None, blk, GLA_HEADS, GLA_DK, GLA_DV), lambda i: (layer, i, 0, 0, 0))
    o = pl.pallas_call(
        _gla_out_sample_kernel,
        grid=(n // blk,),
        in_specs=[vec(GLA_KEY_WIDTH), sspec],
        out_specs=vec(GLA_WIDTH),
        out_shape=jax.ShapeDtypeStruct((n, 1, GLA_WIDTH), jnp.float32),
        compiler_params=pltpu.CompilerParams(
            dimension_semantics=("arbitrary",), vmem_limit_bytes=VMEM_LIMIT),
        name="gla_out_sample",
    )(qd[:, None, :], s_all)
    return o[:, 0, :]


def _gla_update_sample(k_all, a_all, v_all, s_all):
    depth, n = k_all.shape[0], k_all.shape[1]
    blk = SAMPLE_GLA_BLOCK
    vec = lambda w: pl.BlockSpec((None, blk, 1, w), lambda l, i: (l, i, 0, 0))
    sspec = pl.BlockSpec((None, blk, GLA_HEADS, GLA_DK, GLA_DV), lambda l, i: (l, i, 0, 0, 0))
    return pl.pallas_call(
        _gla_update_sample_kernel,
        grid=(depth, n // blk),
        in_specs=[vec(GLA_KEY_WIDTH), vec(GLA_KEY_WIDTH), vec(GLA_WIDTH), sspec],
        out_specs=sspec,
        out_shape=jax.ShapeDtypeStruct(s_all.shape, jnp.float32),
        compiler_params=pltpu.CompilerParams(
            dimension_semantics=("arbitrary", "arbitrary"), vmem_limit_bytes=VMEM_LIMIT),
        name="gla_update_sample",
    )(k_all[:, :, None, :], a_all[:, :, None, :], v_all[:, :, None, :], s_all)


def _tail_sample_kernel(x_ref, oi_ref, sc_ref, v_ref, g_ref, ob_ref, gnw_ref, wout_ref,
                        n2_ref, wup_ref, st0_ref, st1_ref, fw_ref, fb_ref, wdown_ref, nf_ref,
                        xo_ref, up_ref, *, final_norm):
    parts = []
    for hh in range(GLA_HEADS):
        cols = slice(hh * GLA_DV, (hh + 1) * GLA_DV)
        o = oi_ref[:, cols] + sc_ref[:, cols] * v_ref[:, cols]
        o = o * lax.rsqrt(jnp.mean(o * o, axis=-1, keepdims=True) + EPS)
        parts.append(_bf(o * gnw_ref[:, cols] * _silu(g_ref[:, cols])))
    parts.append(_bf(ob_ref[...]))
    x = x_ref[...] + _dot(jnp.concatenate(parts, axis=1), wout_ref[...])

    h2 = _bf(_rms(x, n2_ref[...]))
    up = _dot(h2, wup_ref[...])
    up_ref[0] = st1_ref[...]
    up_ref[1] = up
    upc = (fb_ref[...] + fw_ref[0:1, :] * st0_ref[...] + fw_ref[1:2, :] * st1_ref[...]
           + fw_ref[2:3, :] * up)
    gated = _bf(_silu(upc[:, :D_FF]) * upc[:, D_FF:])
    y = x + _dot(gated, wdown_ref[...])
    if final_norm:
        y = _rms(y, nf_ref[...])
    xo_ref[...] = y


def _tail_sample(x, oi, sc, v, g, ob, ffn_t, layer, p, final_norm):
    n = x.shape[0]
    full = lambda shape: pl.BlockSpec(shape, lambda i: (0,) * len(shape))
    plane = lambda j: pl.BlockSpec((None, None, n, 2 * D_FF), lambda i: (layer, j, 0, 0))
    return pl.pallas_call(
        functools.partial(_tail_sample_kernel, final_norm=final_norm),
        grid=(1,),
        in_specs=[
            full((n, D_MODEL)), full((n, GLA_WIDTH)), full((n, GLA_WIDTH)), full((n, GLA_WIDTH)),
            full((n, GLA_WIDTH)), full((n, CONV_CH)),
            _const_spec((1, GLA_WIDTH), layer),
            _const_spec((D_MODEL, D_MODEL), layer),
            _const_spec((1, D_MODEL), layer),
            _const_spec((D_MODEL, 2 * D_FF), layer),
            plane(0), plane(1),
            _const_spec((FFN_CONV_WIDTH, 2 * D_FF), layer),
            _const_spec((1, 2 * D_FF), layer),
            _const_spec((D_FF, D_MODEL), layer),
            pl.BlockSpec((1, D_MODEL), lambda i: (0, 0)),
        ],
        out_specs=[full((n, D_MODEL)), full((FFN_CONV_WIDTH - 1, n, 2 * D_FF))],
        out_shape=[jax.ShapeDtypeStruct((n, D_MODEL), jnp.float32),
                   jax.ShapeDtypeStruct((FFN_CONV_WIDTH - 1, n, 2 * D_FF), jnp.float32)],
        compiler_params=pltpu.CompilerParams(
            dimension_semantics=("arbitrary",), vmem_limit_bytes=VMEM_LIMIT),
        name="tail_sample",
    )(x, oi, sc, v, g, ob, p["gla_norm_w"], p["w_out"], p["norm2_w"], p["w_up"], ffn_t, ffn_t,
      p["ffn_conv_w"], p["ffn_conv_b"], p["w_down"], p["norm_f_w"])


W_IN_ROWS = 256


def _w_in_layout_kernel(w_ref, o_ref):
    a0 = 2 * GLA_KEY_WIDTH + 2 * GLA_WIDTH
    tail = w_ref[:, a0:]
    o_ref[:, C_U:C_A] = _bf(tail[:, GATE_RANK:])
    lane = lax.broadcasted_iota(jnp.int32, (W_IN_ROWS, LANES), 1)
    o_ref[:, C_A:C_Q] = _bf(jnp.where(lane < GATE_RANK, tail[:, 0:LANES], 0.0))
    o_ref[:, C_Q:IN_PAD] = _bf(w_ref[:, 0:a0])


def _w_in_layout(w_in):
    depth, d, cols = w_in.shape
    return pl.pallas_call(
        _w_in_layout_kernel,
        grid=(depth, d // W_IN_ROWS),
        in_specs=[pl.BlockSpec((None, W_IN_ROWS, cols), lambda l, r: (l, r, 0))],
        out_specs=pl.BlockSpec((None, W_IN_ROWS, IN_PAD), lambda l, r: (l, r, 0)),
        out_shape=jax.ShapeDtypeStruct((depth, d, IN_PAD), jnp.bfloat16),
        compiler_params=pltpu.CompilerParams(dimension_semantics=("arbitrary", "arbitrary")),
        name="w_in_layout",
    )(w_in)


def _prepare_params(norm1_w, w_in, w_gate_up, b_gate, gla_norm_w, conv_w, conv_b, conv_ln_w,
                    conv_ln_b, w_out, norm2_w, w_up, ffn_conv_w, ffn_conv_b, w_down, norm_f_w):
    depth = w_in.shape[0]
    wgu = jnp.concatenate(
        [w_gate_up, jnp.zeros((depth, LANES - GATE_RANK, GLA_KEY_WIDTH), w_gate_up.dtype)], axis=1)
    row = lambda a: a[:, None, :]
    return {
        "norm1_w": row(norm1_w), "w_in": _w_in_layout(w_in), "w_gate_up": _bf(wgu),
        "b_gate": row(b_gate),
        "gla_norm_w": row(gla_norm_w), "conv_w": conv_w, "conv_b": row(conv_b),
        "conv_ln_w": row(conv_ln_w), "conv_ln_b": row(conv_ln_b), "w_out": _bf(w_out),
        "norm2_w": row(norm2_w), "w_up": _bf(w_up), "ffn_conv_w": ffn_conv_w,
        "ffn_conv_b": row(ffn_conv_b), "w_down": _bf(w_down), "norm_f_w": norm_f_w[None, :],
    }


def kernel(x_prompt, x_sample, state_gla, state_conv, state_ffn, norm1_w, w_in, w_gate_up, b_gate,
           gla_norm_w, conv_w, conv_b, conv_ln_w, conv_ln_b, w_out, norm2_w, w_up, ffn_conv_w,
           ffn_conv_b, w_down, norm_f_w):
    p = _prepare_params(norm1_w, w_in, w_gate_up, b_gate, gla_norm_w, conv_w, conv_b, conv_ln_w,
                        conv_ln_b, w_out, norm2_w, w_up, ffn_conv_w, ffn_conv_b, w_down, norm_f_w)
    depth = w_in.shape[0]

    x = x_prompt
    gla_p, conv_p, ffn_p = [], [], []
    for l in range(depth):
        x, s_new, c_buf = _mixer_prompt(x, l, p)
        x, f_buf = _ffn_prompt(x, l, p, final_norm=(l == depth - 1))
        gla_p.append(s_new)
        conv_p.append(c_buf)
        ffn_p.append(f_buf)
    y_prompt = x

    xs = x_sample[:, 0, :]
    conv_t = jnp.swapaxes(state_conv, 1, 2)
    ffn_t = jnp.swapaxes(state_ffn, 1, 2)
    glus, ups, ks, decays, vs = [], [], [], [], []
    for l in range(depth):
        qd, k, a, v, sc, g, glu, ob = _proj_sample(xs, conv_t, l, p)
        oi = _gla_out_sample(qd, state_gla, l)
        xs, up = _tail_sample(xs, oi, sc, v, g, ob, ffn_t, l, p, final_norm=(l == depth - 1))
        glus.append(glu)
        ups.append(up)
        ks.append(k)
        decays.append(a)
        vs.append(v)
    y_sample = xs[:, None, :]
    gla_s = _gla_update_sample(jnp.stack(ks), jnp.stack(decays), jnp.stack(vs), state_gla)
    conv_s = jnp.swapaxes(jnp.stack(glus), 1, 2)
    ffn_s = jnp.swapaxes(jnp.stack(ups), 1, 2)

    return (y_prompt, y_sample, jnp.stack(gla_p), gla_s, jnp.stack(conv_p), conv_s,
            jnp.stack(ffn_p), ffn_s)
```
